```python
import jax, jax.numpy as jnp
from jax import lax
import numpy as np

D_MODEL = 1024
BATCH = 1
SEQ = 16384
DEPTH = 2
DEC_BATCH = 128
DEC_SEQ = 8
PAST_LEN = 16384
PAGE_SIZE = 128

LRU_W = D_MODEL // 4
LRU_BLOCKS = 4
LRU_BW = LRU_W // LRU_BLOCKS
CONV_K = 4
LRU_C = 8.0
RET_H = 6
RET_DK = 32
RET_DV = 64
RET_W = RET_H * RET_DV
RET_CHUNK = 128
MLA_H = 6
MLA_NOPE = 64
MLA_ROPE = 32
MLA_DV = 64
MLA_W = MLA_H * MLA_DV
MLA_Q_LORA = 256
MLA_KV_LORA = 256
Q_BLOCK = 128
D_MIX = LRU_W + RET_W + MLA_W
ROPE_BASE = 10000.0
NORM_EPS = 1e-6
IN_SPLITS = (LRU_W, LRU_W, RET_H * RET_DK, RET_H * RET_DK, RET_W, RET_W, MLA_Q_LORA, MLA_KV_LORA, MLA_ROPE, MLA_W)
D_IN = sum(IN_SPLITS)

kernel_name = 'hybrid_lru_retention_mla_step'

F32 = jnp.float32


def rmsnorm(x, g):
    xf = x.astype(F32)
    return xf * lax.rsqrt(jnp.mean(xf * xf, axis=-1, keepdims=True) + NORM_EPS) * g.astype(F32)


def rope(x, pos):
    half = x.shape[-1] // 2
    inv = ROPE_BASE ** (-jnp.arange(half, dtype=F32) / half)
    ang = pos[:, None] * inv[None, :]
    cos = jnp.cos(ang)[None, :, None, :]
    sin = jnp.sin(ang)[None, :, None, :]
    x1, x2 = x[..., :half], x[..., half:]
    return jnp.concatenate([x1 * cos - x2 * sin, x1 * sin + x2 * cos], axis=-1)


def causal_dwconv(x, buf, w, b):
    T = x.shape[1]
    xp = jnp.concatenate([buf.astype(x.dtype), x], axis=1)
    y = b + xp[:, 0:T] * w[0]
    for k in range(1, CONV_K):
        y = y + xp[:, k:k + T] * w[k]
    return y, xp[:, -(CONV_K - 1):]


def rg_lru(xc, h0, wa, ba, wx, bx, lam):
    B, T, W = xc.shape
    xb = xc.reshape(B, T, LRU_BLOCKS, LRU_BW)
    gate_r = jax.nn.sigmoid(jnp.einsum('btnc,ncd->btnd', xb, wa).reshape(B, T, W) + ba)
    gate_i = jax.nn.sigmoid(jnp.einsum('btnc,ncd->btnd', xb, wx).reshape(B, T, W) + bx)
    log_a = -LRU_C * gate_r * jax.nn.softplus(-lam.astype(F32))
    a = jnp.exp(log_a)
    u = jnp.sqrt(-jnp.expm1(2.0 * log_a)) * (gate_i * xc)
    u = u.at[:, 0].add(a[:, 0] * h0)

    def combine(left, right):
        a1, b1 = left
        a2, b2 = right
        return a1 * a2, a2 * b1 + b2

    _, h = lax.associative_scan(combine, (a, u), axis=1)
    return h, h[:, -1]


def retention(q, k, v, s0):
    B, T = q.shape[0], q.shape[1]
    L = min(RET_CHUNK, T)
    nc = T // L
    log_g = jnp.log1p(-jnp.exp2(-5.0 - jnp.arange(RET_H, dtype=F32)))
    idx = jnp.arange(L, dtype=F32)
    diff = idx[:, None] - idx[None, :]
    intra = jnp.where(diff >= 0, jnp.exp(log_g[:, None, None] * jnp.maximum(diff, 0.0)), 0.0)
    q_dec = jnp.exp(log_g[:, None] * (idx + 1.0))
    k_dec = jnp.exp(log_g[:, None] * (L - 1.0 - idx))
    chunk_dec = jnp.exp(log_g * L)

    def to_chunks(t):
        return t.reshape(B, nc, L, RET_H, t.shape[-1]).transpose(1, 0, 3, 2, 4)

    def step(s, qkv):
        qc, kc, vc = qkv
        scores = jnp.einsum('bhid,bhjd->bhij', qc, kc) * intra
        inner = jnp.einsum('bhij,bhjv->bhiv', scores, vc)
        cross = jnp.einsum('bhid,bhdv->bhiv', qc * q_dec[..., None], s)
        s_new = chunk_dec[:, None, None] * s + jnp.einsum('bhjd,bhjv->bhdv', kc * k_dec[..., None], vc)
        return s_new, inner + cross

    s_last, o = lax.scan(step, s0, (to_chunks(q), to_chunks(k), to_chunks(v)))
    o = o.transpose(1, 0, 3, 2, 4).reshape(B, T, RET_H, RET_DV)
    return o, s_last


def latent_attention(q_lat, q_rope, c_kv, k_rope, q_pos, k_pos):
    B, T = q_lat.shape[0], q_lat.shape[1]
    qb = min(Q_BLOCK, T)
    nb = T // qb
    scale = (MLA_NOPE + MLA_ROPE) ** -0.5

    def blocks(t):
        return jnp.moveaxis(t.reshape(B, nb, qb, *t.shape[2:]), 1, 0)

    def attend(args):
        ql, qr, qp = args
        s = (jnp.einsum('bqhc,bkc->bhqk', ql, c_kv) + jnp.einsum('bqhr,bkr->bhqk', qr, k_rope)) * scale
        s = jnp.where(k_pos[None, :] <= qp[:, None], s, -jnp.inf)
        p = jax.nn.softmax(s.astype(F32), axis=-1)
        return jnp.einsum('bhqk,bkc->bqhc', p, c_kv)

    o = lax.map(attend, (blocks(q_lat), blocks(q_rope), q_pos.reshape(nb, qb)))
    return jnp.moveaxis(o, 0, 1).reshape(B, T, MLA_H, -1)


def mixer_layer(x, pos, conv_buf, h0, s0, past_c, past_kr, past_pos, p):
    B, T, _ = x.shape
    h = rmsnorm(x, p['norm_g']).astype(x.dtype)
    z = jnp.einsum('btd,de->bte', h, p['w_in'])
    split_at = [int(s) for s in np.cumsum(IN_SPLITS)[:-1]]
    (x_lru, g_lru, q_r, k_r, v_r, g_r, q_a, kv_a, kr_a, g_m) = jnp.split(z, split_at, axis=-1)

    xc, new_buf = causal_dwconv(x_lru, conv_buf, p['conv_w'], p['conv_b'])
    hs, h_last = rg_lru(xc.astype(F32), h0.astype(F32), p['lru_wa'], p['lru_ba'], p['lru_wx'], p['lru_bx'], p['lru_lambda'])
    y_a = hs * jax.nn.silu(g_lru.astype(F32))

    q = rope(q_r.astype(F32).reshape(B, T, RET_H, RET_DK), pos)
    k = rope(k_r.astype(F32).reshape(B, T, RET_H, RET_DK), pos) * (RET_DK ** -0.5)
    v = v_r.astype(F32).reshape(B, T, RET_H, RET_DV)
    o, s_last = retention(q, k, v, s0.astype(F32))
    o = rmsnorm(o, p['ret_gn_g'].reshape(RET_H, RET_DV)).reshape(B, T, RET_W)
    y_b = o * jax.nn.silu(g_r.astype(F32))

    cq = rmsnorm(q_a, p['q_norm_g'])
    qh = jnp.einsum('btq,qe->bte', cq, p['w_uq']).reshape(B, T, MLA_H, MLA_NOPE + MLA_ROPE)
    q_nope = qh[..., :MLA_NOPE]
    q_pe = rope(qh[..., MLA_NOPE:], pos)
    c_new = rmsnorm(kv_a, p['kv_norm_g'])
    kr_new = rope(kr_a.astype(F32)[:, :, None, :], pos)[:, :, 0, :]
    q_lat = jnp.einsum('bthn,chn->bthc', q_nope, p['w_uk'])
    if past_c is None:
        c_all, kr_all, k_pos = c_new, kr_new, pos
    else:
        c_all = jnp.concatenate([past_c.astype(F32), c_new], axis=1)
        kr_all = jnp.concatenate([past_kr.astype(F32), kr_new], axis=1)
        k_pos = jnp.concatenate([past_pos, pos], axis=0)
    o_lat = latent_attention(q_lat, q_pe, c_all, kr_all, pos, k_pos)
    y_c = jnp.einsum('bthc,chv->bthv', o_lat, p['w_uv']).reshape(B, T, MLA_W) * jax.nn.silu(g_m.astype(F32))

    mix = jnp.concatenate([y_a, y_b, y_c], axis=-1).astype(x.dtype)
    x = x + jnp.einsum('bte,ed->btd', mix, p['w_out']).astype(x.dtype)
    dt = x.dtype
    return x, (c_new.astype(dt), kr_new.astype(dt), s_last.astype(dt), h_last.astype(dt), new_buf.astype(dt))


def setup_inputs(seed: int = 0) -> dict:
    key = jax.random.key(seed)
    ks = jax.random.split(key, 32)
    n_pages = PAST_LEN // PAGE_SIZE
    n_pool = (DEC_BATCH * n_pages * 5) // 4

    def nrm(k, shape, scale):
        return jax.random.normal(k, shape, F32) * scale

    def gain(k, shape):
        return 1.0 + nrm(k, shape, 0.01)

    u = jax.random.uniform(ks[0], (DEPTH, LRU_W), F32, 0.9, 0.999)
    a = u ** (1.0 / LRU_C)
    perm = jax.random.permutation(ks[1], n_pool)[: DEC_BATCH * n_pages]
    return {
        'x_prompt': nrm(ks[2], (BATCH, SEQ, D_MODEL), 1.0),
        'x_sample': nrm(ks[3], (DEC_BATCH, DEC_SEQ, D_MODEL), 1.0),
        'cache_mla_latent': nrm(ks[4], (DEPTH, n_pool, PAGE_SIZE, MLA_KV_LORA), 1.0),
        'cache_mla_krope': nrm(ks[5], (DEPTH, n_pool, PAGE_SIZE, MLA_ROPE), 1.0),
        'state_ret': nrm(ks[6], (DEPTH, DEC_BATCH, RET_H, RET_DK, RET_DV), 0.5),
        'state_lru_h': nrm(ks[7], (DEPTH, DEC_BATCH, LRU_W), 0.5),
        'state_conv': nrm(ks[8], (DEPTH, DEC_BATCH, CONV_K - 1, LRU_W), 1.0),
        'page_table': perm.reshape(DEC_BATCH, n_pages).astype(jnp.int32),
        'norm_g': gain(ks[9], (DEPTH, D_MODEL)),
        'w_in': nrm(ks[10], (DEPTH, D_MODEL, D_IN), D_MODEL ** -0.5),
        'conv_w': nrm(ks[11], (DEPTH, CONV_K, LRU_W), CONV_K ** -0.5),
        'conv_b': nrm(ks[12], (DEPTH, LRU_W), 0.01),
        'lru_wa': nrm(ks[13], (DEPTH, LRU_BLOCKS, LRU_BW, LRU_BW), LRU_BW ** -0.5),
        'lru_ba': nrm(ks[14], (DEPTH, LRU_W), 0.01),
        'lru_wx': nrm(ks[15], (DEPTH, LRU_BLOCKS, LRU_BW, LRU_BW), LRU_BW ** -0.5),
        'lru_bx': nrm(ks[16], (DEPTH, LRU_W), 0.01),
        'lru_lambda': jnp.log(a) - jnp.log1p(-a),
        'ret_gn_g': gain(ks[17], (DEPTH, RET_W)),
        'q_norm_g': gain(ks[18], (DEPTH, MLA_Q_LORA)),
        'w_uq': nrm(ks[19], (DEPTH, MLA_Q_LORA, MLA_H * (MLA_NOPE + MLA_ROPE)), MLA_Q_LORA ** -0.5),
        'kv_norm_g': gain(ks[20], (DEPTH, MLA_KV_LORA)),
        'w_uk': nrm(ks[21], (DEPTH, MLA_KV_LORA, MLA_H, MLA_NOPE), MLA_KV_LORA ** -0.5),
        'w_uv': nrm(ks[22], (DEPTH, MLA_KV_LORA, MLA_H, MLA_DV), MLA_KV_LORA ** -0.5),
        'w_out': nrm(ks[23], (DEPTH, D_MIX, D_MODEL), D_MIX ** -0.5),
        'final_norm_g': gain(ks[24], (D_MODEL,)),
    }


def reference(x_prompt, x_sample, cache_mla_latent, cache_mla_krope, state_ret, state_lru_h, state_conv, page_table,
              norm_g, w_in, conv_w, conv_b, lru_wa, lru_ba, lru_wx, lru_bx, lru_lambda, ret_gn_g,
              q_norm_g, w_uq, kv_norm_g, w_uk, w_uv, w_out, final_norm_g):
    bp, tp = x_prompt.shape[0], x_prompt.shape[1]
    bs, ts = x_sample.shape[0], x_sample.shape[1]
    past_len = page_table.shape[1] * cache_mla_latent.shape[2]
    pos_p = jnp.arange(tp, dtype=F32)
    past_pos = jnp.arange(past_len, dtype=F32)
    pos_s = past_len + jnp.arange(ts, dtype=F32)

    xp, xs = x_prompt, x_sample
    p_lat, p_kr, p_ret, p_h, p_conv = [], [], [], [], []
    s_lat, s_kr, s_ret, s_h, s_conv = [], [], [], [], []
    for l in range(DEPTH):
        p = dict(norm_g=norm_g[l], w_in=w_in[l], conv_w=conv_w[l], conv_b=conv_b[l],
                 lru_wa=lru_wa[l], lru_ba=lru_ba[l], lru_wx=lru_wx[l], lru_bx=lru_bx[l],
                 lru_lambda=lru_lambda[l], ret_gn_g=ret_gn_g[l], q_norm_g=q_norm_g[l], w_uq=w_uq[l],
                 kv_norm_g=kv_norm_g[l], w_uk=w_uk[l], w_uv=w_uv[l], w_out=w_out[l])
        xp, (c, kr, s, h, buf) = mixer_layer(
            xp, pos_p,
            jnp.zeros((bp, CONV_K - 1, LRU_W), xp.dtype), jnp.zeros((bp, LRU_W), F32),
            jnp.zeros((bp, RET_H, RET_DK, RET_DV), F32), None, None, None, p)
        p_lat.append(c); p_kr.append(kr); p_ret.append(s); p_h.append(h); p_conv.append(buf)
        past_c = cache_mla_latent[l, page_table].reshape(bs, past_len, MLA_KV_LORA)
        past_kr = cache_mla_krope[l, page_table].reshape(bs, past_len, MLA_ROPE)
        xs, (c, kr, s, h, buf) = mixer_layer(
            xs, pos_s, state_conv[l], state_lru_h[l], state_ret[l], past_c, past_kr, past_pos, p)
        s_lat.append(c); s_kr.append(kr); s_ret.append(s); s_h.append(h); s_conv.append(buf)

    y_prompt = rmsnorm(xp, final_norm_g).astype(x_prompt.dtype)
    y_sample = rmsnorm(xs, final_norm_g).astype(x_sample.dtype)
    return (y_prompt, y_sample,
            jnp.stack(p_lat), jnp.stack(p_kr), jnp.stack(p_ret), jnp.stack(p_h), jnp.stack(p_conv),
            jnp.stack(s_lat), jnp.stack(s_kr), jnp.stack(s_ret), jnp.stack(s_h), jnp.stack(s_conv))
```

```python
import functools

import jax
import jax.numpy as jnp
import numpy as np
from jax import lax
from jax.experimental import pallas as pl
from jax.experimental.pallas import tpu as pltpu

F32 = jnp.float32
BF16 = jnp.bfloat16

D_MODEL = 1024
LRU_W = 256
LRU_BLOCKS = 4
CONV_K = 4
LRU_C = 8.0
RET_H = 6
RET_DK = 32
RET_DV = 64
RET_W = RET_H * RET_DV
RET_CHUNK = 128
MLA_H = 6
MLA_NOPE = 64
MLA_ROPE = 32
MLA_DV = 64
MLA_W = MLA_H * MLA_DV
MLA_Q_LORA = 256
MLA_KV_LORA = 256
ROPE_BASE = 10000.0
NORM_EPS = 1e-6

LANES = 128
SUBLANES = 8
HEAD_PAD = 128
QABS_W = 384
VMEM_LIMIT = 48 * 1024 * 1024

Z_XLRU, Z_GLRU, Z_QR, Z_KR, Z_QA, Z_KVA = 0, 256, 512, 768, 1024, 1280
Z_VR, Z_GR, Z_GM, Z_KRA = 1536, 1920, 2304, 2688
Z_W = 2816
RET_QK_PAD = 256


def _params(sem):
    return pltpu.CompilerParams(dimension_semantics=sem, vmem_limit_bytes=VMEM_LIMIT)


def _rms(x, g):
    return x * lax.rsqrt(jnp.mean(x * x, axis=-1, keepdims=True) + NORM_EPS) * g


def _silu(x):
    return x * jax.nn.sigmoid(x)


def _rope(x, c, s):
    n = x.shape[-1]
    lane = lax.broadcasted_iota(jnp.int32, x.shape, x.ndim - 1)
    up = pltpu.roll(x, n - MLA_ROPE // 2, axis=x.ndim - 1)
    dn = pltpu.roll(x, MLA_ROPE // 2, axis=x.ndim - 1)
    partner = jnp.where((lane % MLA_ROPE) < MLA_ROPE // 2, up, dn)
    return x * c + partner * s


def _dot(a, b):
    return jnp.dot(a, b, preferred_element_type=F32)


def _dot_nt(a, b):
    return lax.dot_general(a, b, (((1,), (1,)), ((), ())), preferred_element_type=F32)


def _dot_tn(a, b):
    return lax.dot_general(a, b, (((0,), (0,)), ((), ())), preferred_element_type=F32)


def _in_proj_kernel(x_ref, g_ref, w_ref, z_ref):
    h = _rms(x_ref[...], g_ref[...])
    z_ref[...] = _dot(h.astype(BF16), w_ref[...])


def _in_proj(x2d, g, w_p):
    rows = x2d.shape[0]
    tm = min(512, rows)
    return pl.pallas_call(
        _in_proj_kernel,
        grid=(rows // tm,),
        in_specs=[
            pl.BlockSpec((tm, D_MODEL), lambda i: (i, 0)),
            pl.BlockSpec((1, D_MODEL), lambda i: (0, 0)),
            pl.BlockSpec((D_MODEL, Z_W), lambda i: (0, 0)),
        ],
        out_specs=pl.BlockSpec((tm, Z_W), lambda i: (i, 0)),
        out_shape=jax.ShapeDtypeStruct((rows, Z_W), F32),
        compiler_params=_params(("parallel",)),
        name="in_proj",
    )(x2d, g.reshape(1, D_MODEL), w_p)


def _group_scan(a, u):
    row = lax.broadcasted_iota(jnp.int32, a.shape, 0) % SUBLANES
    d = 1
    while d < SUBLANES:
        keep = row >= d
        a_prev = jnp.where(keep, pltpu.roll(a, d, axis=0), 1.0)
        u_prev = jnp.where(keep, pltpu.roll(u, d, axis=0), 0.0)
        u = a * u_prev + u
        a = a * a_prev
        d *= 2
    return a, u


def _lru_gates(shifted, cw_ref, cb_ref, wa_ref, ba_ref, wx_ref, bx_ref, lam_ref):
    xc = cb_ref[...] + shifted[3] * cw_ref[0:1, :]
    xc = xc + shifted[2] * cw_ref[1:2, :]
    xc = xc + shifted[1] * cw_ref[2:3, :]
    xc = xc + shifted[0] * cw_ref[3:4, :]
    xcb = xc.astype(BF16)
    gate_r = jax.nn.sigmoid(_dot(xcb, wa_ref[...]) + ba_ref[...])
    gate_i = jax.nn.sigmoid(_dot(xcb, wx_ref[...]) + bx_ref[...])
    neg_lam = -lam_ref[...]
    softplus = jnp.maximum(neg_lam, 0.0) + jnp.log1p(jnp.exp(-jnp.abs(neg_lam)))
    log_a = -LRU_C * gate_r * softplus
    a = jnp.exp(log_a)
    u = jnp.sqrt(-jnp.tanh(log_a) * (a * a + 1.0)) * (gate_i * xc)
    return a, u


def _lru_prompt_kernel(x_ref, g_ref, cw_ref, cb_ref, wa_ref, ba_ref, wx_ref, bx_ref, lam_ref,
                       y_ref, hl_ref, xbuf_ref, a_ref, u_ref, h_ref):
    tm = x_ref.shape[0]

    @pl.when(pl.program_id(0) == 0)
    def _():
        xbuf_ref[0:SUBLANES, :] = jnp.zeros((SUBLANES, LRU_W), F32)
        h_ref[...] = jnp.zeros_like(h_ref)

    xbuf_ref[SUBLANES:, :] = x_ref[...]
    xb = xbuf_ref[...]
    shifted = [x_ref[...]] + [pltpu.roll(xb, d, axis=0)[SUBLANES:, :] for d in range(1, CONV_K)]
    xbuf_ref[0:SUBLANES, :] = x_ref[tm - SUBLANES:, :]

    a, u = _lru_gates(shifted, cw_ref, cb_ref, wa_ref, ba_ref, wx_ref, bx_ref, lam_ref)
    a, u = _group_scan(a, u)
    a_ref[...] = a
    u_ref[...] = u

    def body(g, h_prev):
        r0 = pl.multiple_of(g * SUBLANES, SUBLANES)
        h = u_ref[pl.ds(r0, SUBLANES), :] + a_ref[pl.ds(r0, SUBLANES), :] * h_prev
        u_ref[pl.ds(r0, SUBLANES), :] = h
        return jnp.broadcast_to(h[SUBLANES - 1:SUBLANES, :], (SUBLANES, LRU_W))

    h_last = lax.fori_loop(0, tm // SUBLANES, body, h_ref[...])
    h_ref[...] = h_last
    hl_ref[...] = h_last[0:1, :]
    y_ref[...] = u_ref[...] * _silu(g_ref[...])


def _lru_sample_kernel(x_ref, xprev_ref, g_ref, h0_ref, cw_ref, cb_ref, wa_ref, ba_ref, wx_ref, bx_ref,
                       lam_ref, y_ref, hs_ref):
    rows = x_ref.shape[0]
    x = x_ref[...]
    xprev = xprev_ref[...]
    row = lax.broadcasted_iota(jnp.int32, x.shape, 0) % SUBLANES
    shifted = [x] + [jnp.where(row >= d, pltpu.roll(x, d, axis=0), pltpu.roll(xprev, d, axis=0))
                     for d in range(1, CONV_K)]
    a, u = _lru_gates(shifted, cw_ref, cb_ref, wa_ref, ba_ref, wx_ref, bx_ref, lam_ref)
    a, u = _group_scan(a, u)
    nb = rows // SUBLANES
    h = u.reshape(nb, SUBLANES, LRU_W) + a.reshape(nb, SUBLANES, LRU_W) * h0_ref[...][:, None, :]
    h = h.reshape(rows, LRU_W)
    hs_ref[...] = h
    y_ref[...] = h * _silu(g_ref[...])


def _lru_weight_specs():
    full = lambda shape: pl.BlockSpec(shape, lambda *_: (0,) * len(shape))
    return [full((CONV_K, LRU_W)), full((1, LRU_W)), full((LRU_W, LRU_W)), full((1, LRU_W)),
            full((LRU_W, LRU_W)), full((1, LRU_W)), full((1, LRU_W))]


def _lru_prompt(z, lw):
    rows = z.shape[0]
    tm = min(512, rows)
    col = lambda c: pl.BlockSpec((tm, LRU_W), lambda i, c=c: (i, c // LRU_W))
    y, h_last = pl.pallas_call(
        _lru_prompt_kernel,
        grid=(rows // tm,),
        in_specs=[col(Z_XLRU), col(Z_GLRU)] + _lru_weight_specs(),
        out_specs=[pl.BlockSpec((tm, LRU_W), lambda i: (i, 0)), pl.BlockSpec((1, LRU_W), lambda i: (0, 0))],
        out_shape=[jax.ShapeDtypeStruct((rows, LRU_W), F32), jax.ShapeDtypeStruct((1, LRU_W), F32)],
        scratch_shapes=[pltpu.VMEM((tm + SUBLANES, LRU_W), F32), pltpu.VMEM((tm, LRU_W), F32),
                        pltpu.VMEM((tm, LRU_W), F32), pltpu.VMEM((SUBLANES, LRU_W), F32)],
        compiler_params=_params(("arbitrary",)),
        name="lru_prompt",
    )(z, z, *lw)
    return y, h_last


def _lru_sample(z, xprev, h0, lw):
    rows = z.shape[0]
    nb = rows // SUBLANES
    col = lambda c: pl.BlockSpec((rows, LRU_W), lambda i, c=c: (0, c // LRU_W))
    y, hs = pl.pallas_call(
        _lru_sample_kernel,
        grid=(1,),
        in_specs=[col(Z_XLRU), pl.BlockSpec((rows, LRU_W), lambda i: (0, 0)), col(Z_GLRU),
                  pl.BlockSpec((nb, LRU_W), lambda i: (0, 0))] + _lru_weight_specs(),
        out_specs=[pl.BlockSpec((rows, LRU_W), lambda i: (0, 0))] * 2,
        out_shape=[jax.ShapeDtypeStruct((rows, LRU_W), F32)] * 2,
        compiler_params=_params(("arbitrary",)),
        name="lru_sample",
    )(z, xprev, z, h0, *lw)
    return y, hs.reshape(nb, SUBLANES, LRU_W)[:, SUBLANES - 1, :]


def _retention_kernel(q_ref, k_ref, v_ref, g_ref, c_ref, s_ref, s0_ref, intra_ref, qdec_ref, kdec_ref,
                      cdec_ref, gn_ref, y_ref, sl_ref, st_ref):
    ci = pl.program_id(1)

    @pl.when(ci == 0)
    def _():
        st_ref[...] = s0_ref[...]

    cos = c_ref[...]
    sin = s_ref[...]
    q = _rope(q_ref[...], cos, sin)
    k = _rope(k_ref[...], cos, sin) * (RET_DK ** -0.5)
    v = v_ref[...]
    gate = _silu(g_ref[...])
    qd = q * qdec_ref[...]
    kd = k * kdec_ref[...]
    for h in range(RET_H):
        ks = slice(h * RET_DK, (h + 1) * RET_DK)
        vs = slice(h * RET_DV, (h + 1) * RET_DV)
        qh = q[:, ks].astype(BF16)
        kh = k[:, ks].astype(BF16)
        vh = v[:, vs].astype(BF16)
        state = st_ref[h]
        scores = _dot_nt(qh, kh) * intra_ref[h]
        inner = _dot(scores.astype(BF16), vh)
        cross = _dot(qd[:, ks].astype(BF16), state.astype(BF16))
        st_ref[h] = cdec_ref[h] * state + _dot_tn(kd[:, ks].astype(BF16), vh)
        o = inner + cross
        o = o * lax.rsqrt(jnp.mean(o * o, axis=-1, keepdims=True) + NORM_EPS) * gn_ref[:, vs]
        y_ref[:, vs] = o * gate[:, vs]

    @pl.when(ci == pl.num_programs(1) - 1)
    def _():
        sl_ref[...] = st_ref[...]


def _retention(z, cos, sin, s0, consts, gn, batch, seq):
    intra, qdec, kdec, cdec = consts
    chunk = intra.shape[-1]
    nc = seq // chunk
    row = lambda b, c: b * nc + c
    zq = pl.BlockSpec((chunk, RET_QK_PAD), lambda b, c: (row(b, c), Z_QR // RET_QK_PAD))
    zk = pl.BlockSpec((chunk, RET_QK_PAD), lambda b, c: (row(b, c), Z_KR // RET_QK_PAD))
    zv = pl.BlockSpec((chunk, RET_W), lambda b, c: (row(b, c), Z_VR // RET_W))
    zg = pl.BlockSpec((chunk, RET_W), lambda b, c: (row(b, c), Z_GR // RET_W))
    tab = pl.BlockSpec((chunk, RET_QK_PAD), lambda b, c: (c, 0))
    st = pl.BlockSpec((None, RET_H, RET_DK, RET_DV), lambda b, c: (b, 0, 0, 0))
    full = lambda shape: pl.BlockSpec(shape, lambda b, c: (0,) * len(shape))
    y, s_last = pl.pallas_call(
        _retention_kernel,
        grid=(batch, nc),
        in_specs=[zq, zk, zv, zg, tab, tab, st, full(intra.shape), full(qdec.shape), full(kdec.shape),
                  full(cdec.shape), full((1, RET_W))],
        out_specs=[pl.BlockSpec((chunk, RET_W), lambda b, c: (row(b, c), 0)), st],
        out_shape=[jax.ShapeDtypeStruct((batch * seq, RET_W), F32),
                   jax.ShapeDtypeStruct((batch, RET_H, RET_DK, RET_DV), F32)],
        scratch_shapes=[pltpu.VMEM((RET_H, RET_DK, RET_DV), F32)],
        compiler_params=_params(("arbitrary", "arbitrary")),
        name="retention",
    )(z, z, z, z, cos, sin, s0, intra, qdec, kdec, cdec, gn.reshape(1, RET_W))
    return y, s_last


def _retention_consts(seq):
    chunk = min(RET_CHUNK, seq)
    log_g = jnp.log1p(-jnp.exp2(-5.0 - jnp.arange(RET_H, dtype=F32)))
    idx = jnp.arange(chunk, dtype=F32)
    diff = idx[:, None] - idx[None, :]
    intra = jnp.where(diff >= 0, jnp.exp(log_g[:, None, None] * jnp.maximum(diff, 0.0)), 0.0)
    q_dec = jnp.exp(log_g[:, None] * (idx + 1.0))
    k_dec = jnp.exp(log_g[:, None] * (chunk - 1.0 - idx))
    chunk_dec = jnp.exp(log_g * chunk)
    expand = lambda t: jnp.pad(jnp.repeat(t.T, RET_DK, axis=1), ((0, 0), (0, RET_QK_PAD - RET_H * RET_DK)))
    cdec = jnp.broadcast_to(chunk_dec[:, None, None], (RET_H, RET_DK, RET_DV))
    return intra, expand(q_dec), expand(k_dec), cdec


def _mla_common(qa_ref, kva_ref, kra_ref, qg_ref, kvg_ref, wuq_ref, cq_ref, sq_ref, ckr_ref, skr_ref):
    cq = _rms(qa_ref[...], qg_ref[...]).astype(BF16)
    qh = _dot(cq, wuq_ref[...])
    cos = cq_ref[...]
    sin = sq_ref[...]
    scale = (MLA_NOPE + MLA_ROPE) ** -0.5
    q_heads = [_rope(qh[:, h * HEAD_PAD:(h + 1) * HEAD_PAD], cos, sin) * scale for h in range(MLA_H)]
    c_new = _rms(kva_ref[...], kvg_ref[...])
    kr_new = _rope(kra_ref[...], ckr_ref[...], skr_ref[...])
    return q_heads, c_new, kr_new


def _mla_prep_prompt_kernel(qa_ref, kva_ref, kra_ref, qg_ref, kvg_ref, wuq_ref, cq_ref, sq_ref, ckr_ref,
                            skr_ref, wk_ref, wv_ref, c_ref, kr_ref, q_ref, k_ref, v_ref):
    q_heads, c_new, kr_new = _mla_common(qa_ref, kva_ref, kra_ref, qg_ref, kvg_ref, wuq_ref, cq_ref, sq_ref,
                                         ckr_ref, skr_ref)
    c_ref[...] = c_new
    kr_ref[...] = kr_new[:, :MLA_ROPE]
    for h in range(MLA_H):
        q_ref[:, h * HEAD_PAD:(h + 1) * HEAD_PAD] = q_heads[h].astype(BF16)
    ckr = jnp.concatenate([c_new, kr_new], axis=1).astype(BF16)
    k_ref[...] = _dot(ckr, wk_ref[...]).astype(BF16)
    v_ref[...] = _dot(ckr[:, :MLA_KV_LORA], wv_ref[...]).astype(BF16)


def _mla_prep_sample_kernel(qa_ref, kva_ref, kra_ref, qg_ref, kvg_ref, wuq_ref, cq_ref, sq_ref, ckr_ref,
                            skr_ref, wabs_ref, c_ref, kr_ref, q_ref):
    q_heads, c_new, kr_new = _mla_common(qa_ref, kva_ref, kra_ref, qg_ref, kvg_ref, wuq_ref, cq_ref, sq_ref,
                                         ckr_ref, skr_ref)
    c_ref[...] = c_new
    kr_ref[...] = kr_new[:, :MLA_ROPE]
    for h in range(MLA_H):
        q_ref[:, h * QABS_W:(h + 1) * QABS_W] = _dot(q_heads[h].astype(BF16), wabs_ref[h]).astype(BF16)


def _mla_prep(z, tabs, qg, kvg, wuq_p, extra, prompt):
    rows = z.shape[0]
    tm = min(512, rows)
    cq, sq, ckr, skr = tabs
    zc = lambda c, w: pl.BlockSpec((tm, w), lambda i, c=c, w=w: (i, c // w))
    tab = pl.BlockSpec((tm, LANES), lambda i: (i, 0))
    full = lambda shape: pl.BlockSpec(shape, lambda i: (0,) * len(shape))
    in_specs = [zc(Z_QA, MLA_Q_LORA), zc(Z_KVA, MLA_KV_LORA), zc(Z_KRA, LANES), full((1, MLA_Q_LORA)),
                full((1, MLA_KV_LORA)), full(wuq_p.shape), tab, tab, tab, tab] + [full(e.shape) for e in extra]
    rowblk = lambda w: pl.BlockSpec((tm, w), lambda i: (i, 0))
    out_specs = [rowblk(MLA_KV_LORA), rowblk(MLA_ROPE)]
    out_shape = [jax.ShapeDtypeStruct((rows, MLA_KV_LORA), F32), jax.ShapeDtypeStruct((rows, MLA_ROPE), F32)]
    if prompt:
        widths = [MLA_H * HEAD_PAD] * 3
        body = _mla_prep_prompt_kernel
    else:
        widths = [MLA_H * QABS_W]
        body = _mla_prep_sample_kernel
    out_specs += [rowblk(w) for w in widths]
    out_shape += [jax.ShapeDtypeStruct((rows, w), BF16) for w in widths]
    return pl.pallas_call(
        body,
        grid=(rows // tm,),
        in_specs=in_specs,
        out_specs=out_specs,
        out_shape=out_shape,
        compiler_params=_params(("parallel",)),
        name="mla_prep_prompt" if prompt else "mla_prep_sample",
    )(z, z, z, qg.reshape(1, -1), kvg.reshape(1, -1), wuq_p, cq, sq, ckr, skr, *extra)


def _flash_kernel(q_ref, k_ref, v_ref, o_ref, m_ref, l_ref, acc_ref):
    qi = pl.program_id(0)
    ki = pl.program_id(1)
    tq = q_ref.shape[0]
    tk = k_ref.shape[0]

    @pl.when(ki == 0)
    def _():
        m_ref[...] = jnp.full_like(m_ref, -jnp.inf)
        l_ref[...] = jnp.zeros_like(l_ref)
        acc_ref[...] = jnp.zeros_like(acc_ref)

    def update(masked):
        if masked:
            keep = (lax.broadcasted_iota(jnp.int32, (tq, tk), 0) >= lax.broadcasted_iota(jnp.int32, (tq, tk), 1))
        for h in range(MLA_H):
            hs = slice(h * HEAD_PAD, (h + 1) * HEAD_PAD)
            s = _dot_nt(q_ref[:, hs], k_ref[:, hs])
            if masked:
                s = jnp.where(keep, s, -jnp.inf)
            m_old = m_ref[h]
            m_new = jnp.maximum(m_old, jnp.max(s, axis=1, keepdims=True))
            p = jnp.exp(s - m_new)
            alpha = jnp.exp(m_old - m_new)
            l_ref[h] = alpha * l_ref[h] + jnp.sum(p, axis=1, keepdims=True)
            acc_ref[h] = alpha * acc_ref[h] + _dot(p.astype(BF16), v_ref[:, hs])
            m_ref[h] = m_new

    @pl.when(ki < qi)
    def _():
        update(False)

    @pl.when(ki == qi)
    def _():
        update(True)
        for h in range(MLA_H):
            o_ref[:, h * MLA_DV:(h + 1) * MLA_DV] = (acc_ref[h] / l_ref[h])[:, :MLA_DV]


def _flash(q, k, v):
    rows = q.shape[0]
    t = min(512, rows)
    n = rows // t
    w = MLA_H * HEAD_PAD
    return pl.pallas_call(
        _flash_kernel,
        grid=(n, n),
        in_specs=[pl.BlockSpec((t, w), lambda i, j: (i, 0)),
                  pl.BlockSpec((t, w), lambda i, j: (jnp.minimum(i, j), 0)),
                  pl.BlockSpec((t, w), lambda i, j: (jnp.minimum(i, j), 0))],
        out_specs=pl.BlockSpec((t, MLA_W), lambda i, j: (i, 0)),
        out_shape=jax.ShapeDtypeStruct((rows, MLA_W), F32),
        scratch_shapes=[pltpu.VMEM((MLA_H, t, 1), F32), pltpu.VMEM((MLA_H, t, 1), F32),
                        pltpu.VMEM((MLA_H, t, HEAD_PAD), F32)],
        compiler_params=_params(("parallel", "arbitrary")),
        name="flash_prompt",
    )(q, k, v)


def _paged_kernel(pages, pt_ref, q_ref, cn_ref, krn_ref, *refs):
    lat_refs = refs[:pages]
    kr_refs = refs[pages:2 * pages]
    o_ref, m_ref, l_ref, acc_ref, cbuf_ref, krbuf_ref = refs[2 * pages:]
    ji = pl.program_id(1)
    page = lat_refs[0].shape[0]
    rows = q_ref.shape[0]

    @pl.when(ji == 0)
    def _():
        m_ref[...] = jnp.full_like(m_ref, -jnp.inf)
        l_ref[...] = jnp.zeros_like(l_ref)
        acc_ref[...] = jnp.zeros_like(acc_ref)

    for i in range(pages):
        cbuf_ref[i * page:(i + 1) * page, :] = lat_refs[i][...].astype(BF16)
        krbuf_ref[i * page:(i + 1) * page, :] = kr_refs[i][...].astype(BF16)

    q_lat = q_ref[:, :MLA_KV_LORA]
    q_pe = q_ref[:, MLA_KV_LORA:MLA_KV_LORA + MLA_ROPE]

    def update(s, values):
        m_old = m_ref[...]
        m_new = jnp.maximum(m_old, jnp.max(s, axis=1, keepdims=True))
        p = jnp.exp(s - m_new)
        alpha = jnp.exp(m_old - m_new)
        l_ref[...] = alpha * l_ref[...] + jnp.sum(p, axis=1, keepdims=True)
        acc_ref[...] = alpha * acc_ref[...] + _dot(p.astype(BF16), values)
        m_ref[...] = m_new

    c = cbuf_ref[...]
    update(_dot_nt(q_lat, c) + _dot_nt(q_pe, krbuf_ref[...]), c)

    @pl.when(ji == pl.num_programs(1) - 1)
    def _():
        c_new = cn_ref[...].astype(BF16)
        s = _dot_nt(q_lat, c_new) + _dot_nt(q_pe, krn_ref[...].astype(BF16))
        t_q = lax.broadcasted_iota(jnp.int32, s.shape, 0) % SUBLANES
        t_k = lax.broadcasted_iota(jnp.int32, s.shape, 1)
        update(jnp.where(t_k <= t_q, s, -jnp.inf), c_new)
        o_ref[...] = acc_ref[...] / l_ref[...]


def _paged_attention(layer, q, c_new, kr_new, cache_lat, cache_kr, page_table):
    batch, rows, _ = q.shape
    t_new = c_new.shape[1]
    n_pages = page_table.shape[1]
    page = cache_lat.shape[2]
    pages = min(16, n_pages)
    nch = n_pages // pages

    def page_spec(width, i):
        return pl.BlockSpec((None, None, page, width),
                            lambda b, j, pt, i=i: (layer, pt[b * n_pages + j * pages + i], 0, 0))

    per_b = lambda r, w: pl.BlockSpec((None, r, w), lambda b, j, pt: (b, 0, 0))
    grid_spec = pltpu.PrefetchScalarGridSpec(
        num_scalar_prefetch=1,
        grid=(batch, nch),
        in_specs=[per_b(rows, QABS_W), per_b(t_new, MLA_KV_LORA), per_b(t_new, MLA_ROPE)]
        + [page_spec(MLA_KV_LORA, i) for i in range(pages)] + [page_spec(MLA_ROPE, i) for i in range(pages)],
        out_specs=per_b(rows, MLA_KV_LORA),
        scratch_shapes=[pltpu.VMEM((rows, 1), F32), pltpu.VMEM((rows, 1), F32),
                        pltpu.VMEM((rows, MLA_KV_LORA), F32),
                        pltpu.VMEM((pages * page, MLA_KV_LORA), BF16), pltpu.VMEM((pages * page, MLA_ROPE), BF16)],
    )
    return pl.pallas_call(
        functools.partial(_paged_kernel, pages),
        grid_spec=grid_spec,
        out_shape=jax.ShapeDtypeStruct((batch, rows, MLA_KV_LORA), F32),
        compiler_params=_params(("parallel", "arbitrary")),
        name="paged_sample",
    )(page_table.reshape(-1), q, c_new, kr_new, *([cache_lat] * pages), *([cache_kr] * pages))


def _uv_kernel(o_ref, w_ref, y_ref):
    nb = o_ref.shape[0]
    t = o_ref.shape[2]
    for h in range(MLA_H):
        o_h = o_ref[:, h].reshape(nb * t, MLA_KV_LORA).astype(BF16)
        y_ref[:, h * MLA_DV:(h + 1) * MLA_DV] = _dot(o_h, w_ref[h])


def _uv_proj(o_lat, w_uv_h):
    batch, _, t, _ = o_lat.shape
    return pl.pallas_call(
        _uv_kernel,
        grid=(1,),
        in_specs=[pl.BlockSpec(o_lat.shape, lambda i: (0, 0, 0, 0)),
                  pl.BlockSpec(w_uv_h.shape, lambda i: (0, 0, 0))],
        out_specs=pl.BlockSpec((batch * t, MLA_W), lambda i: (0, 0)),
        out_shape=jax.ShapeDtypeStruct((batch * t, MLA_W), F32),
        compiler_params=_params(("arbitrary",)),
        name="uv_proj",
    )(o_lat, w_uv_h)


def _out_proj_kernel(final, x_ref, ya_ref, yb_ref, oc_ref, gm_ref, w_ref, fg_ref, *out_refs):
    yc = oc_ref[...] * _silu(gm_ref[...])
    upd = _dot(ya_ref[...].astype(BF16), w_ref[0:LRU_W, :])
    upd = upd + _dot(yb_ref[...].astype(BF16), w_ref[LRU_W:LRU_W + RET_W, :])
    upd = upd + _dot(yc.astype(BF16), w_ref[LRU_W + RET_W:, :])
    x_new = x_ref[...] + upd
    out_refs[0][...] = x_new
    if final:
        out_refs[1][...] = _rms(x_new, fg_ref[...])


def _out_proj(x2d, y_a, y_b, o_c, z, w_out, final_g, final):
    rows = x2d.shape[0]
    tm = min(512, rows)
    rowblk = lambda w: pl.BlockSpec((tm, w), lambda i: (i, 0))
    full = lambda shape: pl.BlockSpec(shape, lambda i: (0,) * len(shape))
    n_out = 2 if final else 1
    outs = pl.pallas_call(
        functools.partial(_out_proj_kernel, final),
        grid=(rows // tm,),
        in_specs=[rowblk(D_MODEL), rowblk(LRU_W), rowblk(RET_W), rowblk(MLA_W),
                  pl.BlockSpec((tm, MLA_W), lambda i: (i, Z_GM // MLA_W)), full(w_out.shape), full((1, D_MODEL))],
        out_specs=[rowblk(D_MODEL)] * n_out,
        out_shape=[jax.ShapeDtypeStruct((rows, D_MODEL), F32)] * n_out,
        compiler_params=_params(("parallel",)),
        name="out_proj_final" if final else "out_proj",
    )(x2d, y_a, y_b, o_c, z, w_out, final_g.reshape(1, D_MODEL))
    return outs


def _pad_cols(w, width):
    return jnp.pad(w, ((0, 0), (0, width - w.shape[1])))


def _layout_w_in(w):
    splits = np.cumsum([0, LRU_W, LRU_W, RET_H * RET_DK, RET_H * RET_DK, RET_W, RET_W, MLA_Q_LORA, MLA_KV_LORA,
                        MLA_ROPE, MLA_W])
    seg = [w[:, splits[i]:splits[i + 1]] for i in range(10)]
    x_lru, g_lru, q_r, k_r, v_r, g_r, q_a, kv_a, kr_a, g_m = seg
    cols = [x_lru, g_lru, _pad_cols(q_r, RET_QK_PAD), _pad_cols(k_r, RET_QK_PAD), q_a, kv_a, v_r, g_r, g_m,
            _pad_cols(kr_a, LANES)]
    return jnp.concatenate(cols, axis=1).astype(BF16)


def _block_diag(w):
    n, c, d = w.shape
    eye = jnp.eye(n, dtype=w.dtype)
    return (eye[:, None, :, None] * w[:, :, None, :]).reshape(n * c, n * d)


def _layout_w_uq(w):
    per_head = w.reshape(MLA_Q_LORA, MLA_H, MLA_NOPE + MLA_ROPE)
    per_head = jnp.pad(per_head, ((0, 0), (0, 0), (0, HEAD_PAD - MLA_NOPE - MLA_ROPE)))
    return per_head.reshape(MLA_Q_LORA, MLA_H * HEAD_PAD).astype(BF16)


def _layout_w_key(w_uk):
    top = jnp.pad(w_uk, ((0, 0), (0, 0), (0, HEAD_PAD - MLA_NOPE)))
    rope_rows = jnp.zeros((LANES, MLA_H, HEAD_PAD), F32)
    eye = jnp.eye(MLA_ROPE, dtype=F32)
    rope_rows = rope_rows.at[:MLA_ROPE, :, MLA_NOPE:MLA_NOPE + MLA_ROPE].set(
        jnp.broadcast_to(eye[:, None, :], (MLA_ROPE, MLA_H, MLA_ROPE)))
    return jnp.concatenate([top, rope_rows], axis=0).reshape(MLA_KV_LORA + LANES, MLA_H * HEAD_PAD).astype(BF16)


def _layout_w_val(w_uv):
    return jnp.pad(w_uv, ((0, 0), (0, 0), (0, HEAD_PAD - MLA_DV))).reshape(MLA_KV_LORA, MLA_H * HEAD_PAD).astype(BF16)


def _layout_w_abs(w_uk):
    w = jnp.zeros((MLA_H, HEAD_PAD, QABS_W), F32)
    w = w.at[:, :MLA_NOPE, :MLA_KV_LORA].set(w_uk.transpose(1, 2, 0))
    w = w.at[:, MLA_NOPE:MLA_NOPE + MLA_ROPE, MLA_KV_LORA:MLA_KV_LORA + MLA_ROPE].set(
        jnp.broadcast_to(jnp.eye(MLA_ROPE, dtype=F32), (MLA_H, MLA_ROPE, MLA_ROPE)))
    return w.astype(BF16)


def _rope_tables(pos):
    half = MLA_ROPE // 2
    inv = ROPE_BASE ** (-jnp.arange(half, dtype=F32) / half)
    ang = pos[:, None] * inv[None, :]
    cos = jnp.cos(ang)
    sin = jnp.sin(ang)
    c32 = jnp.concatenate([cos, cos], axis=1)
    s32 = jnp.concatenate([-sin, sin], axis=1)
    t = pos.shape[0]
    ret_c = _pad_cols(jnp.tile(c32, (1, RET_H)), RET_QK_PAD)
    ret_s = _pad_cols(jnp.tile(s32, (1, RET_H)), RET_QK_PAD)
    q_c = jnp.concatenate([jnp.ones((t, MLA_NOPE), F32), c32, jnp.zeros((t, HEAD_PAD - MLA_NOPE - MLA_ROPE), F32)], axis=1)
    q_s = jnp.concatenate([jnp.zeros((t, MLA_NOPE), F32), s32, jnp.zeros((t, HEAD_PAD - MLA_NOPE - MLA_ROPE), F32)], axis=1)
    kr_c = _pad_cols(c32, LANES)
    kr_s = _pad_cols(s32, LANES)
    return (ret_c, ret_s), (q_c, q_s, kr_c, kr_s)


def kernel(x_prompt, x_sample, cache_mla_latent, cache_mla_krope, state_ret, state_lru_h, state_conv, page_table,
           norm_g, w_in, conv_w, conv_b, lru_wa, lru_ba, lru_wx, lru_bx, lru_lambda, ret_gn_g, q_norm_g, w_uq,
           kv_norm_g, w_uk, w_uv, w_out, final_norm_g):
    bp, tp, _ = x_prompt.shape
    bs, ts, _ = x_sample.shape
    depth = w_in.shape[0]
    past_len = page_table.shape[1] * cache_mla_latent.shape[2]
    assert bp == 1 and ts == SUBLANES

    pos_p = jnp.arange(tp, dtype=F32)
    pos_s = past_len + jnp.arange(ts, dtype=F32)
    ret_tab_p, mla_tab_p = _rope_tables(pos_p)
    ret_tab_s, mla_tab_s = _rope_tables(pos_s)
    ret_const_p = _retention_consts(tp)
    ret_const_s = _retention_consts(ts)

    xp = x_prompt.reshape(bp * tp, D_MODEL)
    xs = x_sample.reshape(bs * ts, D_MODEL)
    yp = ys = None
    outs = [[] for _ in range(10)]
    for l in range(depth):
        final = l == depth - 1
        w_in_p = _layout_w_in(w_in[l])
        lw = (conv_w[l], conv_b[l].reshape(1, -1), _block_diag(lru_wa[l]).astype(BF16), lru_ba[l].reshape(1, -1),
              _block_diag(lru_wx[l]).astype(BF16), lru_bx[l].reshape(1, -1), lru_lambda[l].reshape(1, -1))
        wuq_p = _layout_w_uq(w_uq[l])
        w_out_b = w_out[l].astype(BF16)

        z = _in_proj(xp, norm_g[l], w_in_p)
        y_a, h_last = _lru_prompt(z, lw)
        y_b, s_last = _retention(z, *ret_tab_p, jnp.zeros((bp, RET_H, RET_DK, RET_DV), F32), ret_const_p,
                                 ret_gn_g[l], bp, tp)
        c_new, kr_new, q, k, v = _mla_prep(z, mla_tab_p, q_norm_g[l], kv_norm_g[l], wuq_p,
                                           (_layout_w_key(w_uk[l]), _layout_w_val(w_uv[l])), True)
        o_c = _flash(q, k, v)
        res = _out_proj(xp, y_a, y_b, o_c, z, w_out_b, final_norm_g, final)
        xp = res[0]
        if final:
            yp = res[1]
        conv_rows = z[:, Z_XLRU:Z_XLRU + LRU_W].reshape(bp, tp, LRU_W)[:, tp - (CONV_K - 1):, :]
        for lst, val in zip(outs[:5], (c_new.reshape(bp, tp, -1), kr_new.reshape(bp, tp, -1), s_last,
                                       h_last.reshape(bp, LRU_W), conv_rows)):
            lst.append(val)

        z = _in_proj(xs, norm_g[l], w_in_p)
        buf = jnp.pad(state_conv[l], ((0, 0), (SUBLANES - (CONV_K - 1), 0), (0, 0)))
        xprev = jnp.roll(buf, -1, axis=0).reshape(bs * ts, LRU_W)
        y_a, h_last = _lru_sample(z, xprev, state_lru_h[l], lw)
        y_b, s_last = _retention(z, *ret_tab_s, state_ret[l], ret_const_s, ret_gn_g[l], bs, ts)
        tabs_s = tuple(jnp.tile(t, (bs, 1)) for t in mla_tab_s)
        c_new, kr_new, q_abs = _mla_prep(z, tabs_s, q_norm_g[l], kv_norm_g[l], wuq_p, (_layout_w_abs(w_uk[l]),), False)
        q_abs = q_abs.reshape(bs, ts, MLA_H, QABS_W).transpose(0, 2, 1, 3).reshape(bs, MLA_H * ts, QABS_W)
        o_lat = _paged_attention(l, q_abs, c_new.reshape(bs, ts, -1), kr_new.reshape(bs, ts, -1),
                                 cache_mla_latent, cache_mla_krope, page_table)
        o_c = _uv_proj(o_lat.reshape(bs, MLA_H, ts, MLA_KV_LORA), w_uv[l].transpose(1, 0, 2).astype(BF16))
        res = _out_proj(xs, y_a, y_b, o_c, z, w_out_b, final_norm_g, final)
        xs = res[0]
        if final:
            ys = res[1]
        conv_rows = z[:, Z_XLRU:Z_XLRU + LRU_W].reshape(bs, ts, LRU_W)[:, ts - (CONV_K - 1):, :]
        for lst, val in zip(outs[5:], (c_new.reshape(bs, ts, -1), kr_new.reshape(bs, ts, -1), s_last, h_last,
                                       conv_rows)):
            lst.append(val)

    return (yp.reshape(bp, tp, D_MODEL), ys.reshape(bs, ts, D_MODEL)) + tuple(jnp.stack(o) for o in outs)
```

```python
import functools

import jax
import jax.numpy as jnp
import numpy as np
from jax import lax
from jax.experimental import pallas as pl
from jax.experimental.pallas import tpu as pltpu

F32 = jnp.float32
BF16 = jnp.bfloat16

D_MODEL = 1024
LRU_W = 256
LRU_BLOCKS = 4
CONV_K = 4
LRU_C = 8.0
RET_H = 6
RET_DK = 32
RET_DV = 64
RET_W = RET_H * RET_DV
RET_CHUNK = 128
MLA_H = 6
MLA_NOPE = 64
MLA_ROPE = 32
MLA_DV = 64
MLA_W = MLA_H * MLA_DV
MLA_Q_LORA = 256
MLA_KV_LORA = 256
ROPE_BASE = 10000.0
NORM_EPS = 1e-6
LOG2_E = 1.4426950408889634

LANES = 128
SUBLANES = 8
HEAD_PAD = 128
QABS_W = 384
VMEM_LIMIT = 48 * 1024 * 1024

Z_XLRU, Z_GLRU, Z_QR, Z_KR, Z_QA, Z_KVA = 0, 256, 512, 768, 1024, 1280
Z_VR, Z_GR, Z_GM, Z_KRA = 1536, 1920, 2304, 2688
Z_W = 2816
RET_QK_PAD = 256


def _params(sem):
    return pltpu.CompilerParams(dimension_semantics=sem, vmem_limit_bytes=VMEM_LIMIT)


def _rms(x, g):
    return x * lax.rsqrt(jnp.mean(x * x, axis=-1, keepdims=True) + NORM_EPS) * g


def _silu(x):
    return x * jax.nn.sigmoid(x)


def _rope(x, c, s):
    n = x.shape[-1]
    lane = lax.broadcasted_iota(jnp.int32, x.shape, x.ndim - 1)
    up = pltpu.roll(x, n - MLA_ROPE // 2, axis=x.ndim - 1)
    dn = pltpu.roll(x, MLA_ROPE // 2, axis=x.ndim - 1)
    partner = jnp.where((lane % MLA_ROPE) < MLA_ROPE // 2, up, dn)
    return x * c + partner * s


def _dot(a, b):
    return jnp.dot(a, b, preferred_element_type=F32)


def _dot_nt(a, b):
    return lax.dot_general(a, b, (((1,), (1,)), ((), ())), preferred_element_type=F32)


def _dot_tn(a, b):
    return lax.dot_general(a, b, (((0,), (0,)), ((), ())), preferred_element_type=F32)


def _in_proj_kernel(x_ref, g_ref, w_ref, z_ref):
    h = _rms(x_ref[...], g_ref[...])
    z_ref[...] = _dot(h.astype(BF16), w_ref[...])


def _in_proj(x2d, g, w_p):
    rows = x2d.shape[0]
    tm = min(512, rows)
    return pl.pallas_call(
        _in_proj_kernel,
        grid=(rows // tm,),
        in_specs=[
            pl.BlockSpec((tm, D_MODEL), lambda i: (i, 0)),
            pl.BlockSpec((1, D_MODEL), lambda i: (0, 0)),
            pl.BlockSpec((D_MODEL, Z_W), lambda i: (0, 0)),
        ],
        out_specs=pl.BlockSpec((tm, Z_W), lambda i: (i, 0)),
        out_shape=jax.ShapeDtypeStruct((rows, Z_W), F32),
        compiler_params=_params(("parallel",)),
        name="in_proj",
    )(x2d, g.reshape(1, D_MODEL), w_p)


def _group_scan(a, u):
    row = lax.broadcasted_iota(jnp.int32, a.shape, 0) % SUBLANES
    d = 1
    while d < SUBLANES:
        keep = row >= d
        a_prev = jnp.where(keep, pltpu.roll(a, d, axis=0), 1.0)
        u_prev = jnp.where(keep, pltpu.roll(u, d, axis=0), 0.0)
        u = a * u_prev + u
        a = a * a_prev
        d *= 2
    return a, u


def _lru_gates(shifted, cw_ref, cb_ref, wa_ref, ba_ref, wx_ref, bx_ref, lam_ref):
    xc = cb_ref[...] + shifted[3] * cw_ref[0:1, :]
    xc = xc + shifted[2] * cw_ref[1:2, :]
    xc = xc + shifted[1] * cw_ref[2:3, :]
    xc = xc + shifted[0] * cw_ref[3:4, :]
    xcb = xc.astype(BF16)
    gate_r = jax.nn.sigmoid(_dot(xcb, wa_ref[...]) + ba_ref[...])
    gate_i = jax.nn.sigmoid(_dot(xcb, wx_ref[...]) + bx_ref[...])
    neg_lam = -lam_ref[...]
    softplus = jnp.maximum(neg_lam, 0.0) + jnp.log1p(jnp.exp(-jnp.abs(neg_lam)))
    log_a = -LRU_C * gate_r * softplus
    a = jnp.exp(log_a)
    u = jnp.sqrt(-jnp.tanh(log_a) * (a * a + 1.0)) * (gate_i * xc)
    return a, u


def _lru_prompt_kernel(x_ref, g_ref, cw_ref, cb_ref, wa_ref, ba_ref, wx_ref, bx_ref, lam_ref,
                       y_ref, hl_ref, xbuf_ref, a_ref, u_ref, h_ref):
    tm = x_ref.shape[0]

    @pl.when(pl.program_id(0) == 0)
    def _():
        xbuf_ref[0:SUBLANES, :] = jnp.zeros((SUBLANES, LRU_W), F32)
        h_ref[...] = jnp.zeros_like(h_ref)

    xbuf_ref[SUBLANES:, :] = x_ref[...]
    xb = xbuf_ref[...]
    shifted = [x_ref[...]] + [pltpu.roll(xb, d, axis=0)[SUBLANES:, :] for d in range(1, CONV_K)]
    xbuf_ref[0:SUBLANES, :] = x_ref[tm - SUBLANES:, :]

    a, u = _lru_gates(shifted, cw_ref, cb_ref, wa_ref, ba_ref, wx_ref, bx_ref, lam_ref)
    a, u = _group_scan(a, u)
    a_ref[...] = a
    u_ref[...] = u

    def body(g, h_prev):
        r0 = pl.multiple_of(g * SUBLANES, SUBLANES)
        h = u_ref[pl.ds(r0, SUBLANES), :] + a_ref[pl.ds(r0, SUBLANES), :] * h_prev
        u_ref[pl.ds(r0, SUBLANES), :] = h
        return jnp.broadcast_to(h[SUBLANES - 1:SUBLANES, :], (SUBLANES, LRU_W))

    h_last = lax.fori_loop(0, tm // SUBLANES, body, h_ref[...])
    h_ref[...] = h_last
    hl_ref[...] = h_last[0:1, :]
    y_ref[...] = u_ref[...] * _silu(g_ref[...])


def _lru_sample_kernel(x_ref, xprev_ref, g_ref, h0_ref, cw_ref, cb_ref, wa_ref, ba_ref, wx_ref, bx_ref,
                       lam_ref, y_ref, hs_ref):
    rows = x_ref.shape[0]
    x = x_ref[...]
    xprev = xprev_ref[...]
    row = lax.broadcasted_iota(jnp.int32, x.shape, 0) % SUBLANES
    shifted = [x] + [jnp.where(row >= d, pltpu.roll(x, d, axis=0), pltpu.roll(xprev, d, axis=0))
                     for d in range(1, CONV_K)]
    a, u = _lru_gates(shifted, cw_ref, cb_ref, wa_ref, ba_ref, wx_ref, bx_ref, lam_ref)
    a, u = _group_scan(a, u)
    nb = rows // SUBLANES
    h = u.reshape(nb, SUBLANES, LRU_W) + a.reshape(nb, SUBLANES, LRU_W) * h0_ref[...][:, None, :]
    h = h.reshape(rows, LRU_W)
    hs_ref[...] = h
    y_ref[...] = h * _silu(g_ref[...])


def _lru_weight_specs():
    full = lambda shape: pl.BlockSpec(shape, lambda *_: (0,) * len(shape))
    return [full((CONV_K, LRU_W)), full((1, LRU_W)), full((LRU_W, LRU_W)), full((1, LRU_W)),
            full((LRU_W, LRU_W)), full((1, LRU_W)), full((1, LRU_W))]


def _lru_prompt(z, lw):
    rows = z.shape[0]
    tm = min(512, rows)
    col = lambda c: pl.BlockSpec((tm, LRU_W), lambda i, c=c: (i, c // LRU_W))
    y, h_last = pl.pallas_call(
        _lru_prompt_kernel,
        grid=(rows // tm,),
        in_specs=[col(Z_XLRU), col(Z_GLRU)] + _lru_weight_specs(),
        out_specs=[pl.BlockSpec((tm, LRU_W), lambda i: (i, 0)), pl.BlockSpec((1, LRU_W), lambda i: (0, 0))],
        out_shape=[jax.ShapeDtypeStruct((rows, LRU_W), F32), jax.ShapeDtypeStruct((1, LRU_W), F32)],
        scratch_shapes=[pltpu.VMEM((tm + SUBLANES, LRU_W), F32), pltpu.VMEM((tm, LRU_W), F32),
                        pltpu.VMEM((tm, LRU_W), F32), pltpu.VMEM((SUBLANES, LRU_W), F32)],
        compiler_params=_params(("arbitrary",)),
        name="lru_prompt",
    )(z, z, *lw)
    return y, h_last


def _lru_sample(z, xprev, h0, lw):
    rows = z.shape[0]
    nb = rows // SUBLANES
    col = lambda c: pl.BlockSpec((rows, LRU_W), lambda i, c=c: (0, c // LRU_W))
    y, hs = pl.pallas_call(
        _lru_sample_kernel,
        grid=(1,),
        in_specs=[col(Z_XLRU), pl.BlockSpec((rows, LRU_W), lambda i: (0, 0)), col(Z_GLRU),
                  pl.BlockSpec((nb, LRU_W), lambda i: (0, 0))] + _lru_weight_specs(),
        out_specs=[pl.BlockSpec((rows, LRU_W), lambda i: (0, 0))] * 2,
        out_shape=[jax.ShapeDtypeStruct((rows, LRU_W), F32)] * 2,
        compiler_params=_params(("arbitrary",)),
        name="lru_sample",
    )(z, xprev, z, h0, *lw)
    return y, hs.reshape(nb, SUBLANES, LRU_W)[:, SUBLANES - 1, :]


def _retention_kernel(q_ref, k_ref, v_ref, g_ref, c_ref, s_ref, s0_ref, intra_ref, qdec_ref, kdec_ref,
                      cdec_ref, gn_ref, y_ref, sl_ref, st_ref):
    ci = pl.program_id(1)

    @pl.when(ci == 0)
    def _():
        st_ref[...] = s0_ref[...]

    cos = c_ref[...]
    sin = s_ref[...]
    q = _rope(q_ref[...], cos, sin)
    k = _rope(k_ref[...], cos, sin) * (RET_DK ** -0.5)
    v = v_ref[...]
    gate = _silu(g_ref[...])
    qd = q * qdec_ref[...]
    kd = k * kdec_ref[...]
    for h in range(RET_H):
        ks = slice(h * RET_DK, (h + 1) * RET_DK)
        vs = slice(h * RET_DV, (h + 1) * RET_DV)
        qh = q[:, ks].astype(BF16)
        kh = k[:, ks].astype(BF16)
        vh = v[:, vs].astype(BF16)
        state = st_ref[h]
        scores = _dot_nt(qh, kh) * intra_ref[h]
        inner = _dot(scores.astype(BF16), vh)
        cross = _dot(qd[:, ks].astype(BF16), state.astype(BF16))
        st_ref[h] = cdec_ref[h] * state + _dot_tn(kd[:, ks].astype(BF16), vh)
        o = inner + cross
        o = o * lax.rsqrt(jnp.mean(o * o, axis=-1, keepdims=True) + NORM_EPS) * gn_ref[:, vs]
        y_ref[:, vs] = o * gate[:, vs]

    @pl.when(ci == pl.num_programs(1) - 1)
    def _():
        sl_ref[...] = st_ref[...]


def _retention(z, cos, sin, s0, consts, gn, batch, seq):
    intra, qdec, kdec, cdec = consts
    chunk = intra.shape[-1]
    nc = seq // chunk
    row = lambda b, c: b * nc + c
    zq = pl.BlockSpec((chunk, RET_QK_PAD), lambda b, c: (row(b, c), Z_QR // RET_QK_PAD))
    zk = pl.BlockSpec((chunk, RET_QK_PAD), lambda b, c: (row(b, c), Z_KR // RET_QK_PAD))
    zv = pl.BlockSpec((chunk, RET_W), lambda b, c: (row(b, c), Z_VR // RET_W))
    zg = pl.BlockSpec((chunk, RET_W), lambda b, c: (row(b, c), Z_GR // RET_W))
    tab = pl.BlockSpec((chunk, RET_QK_PAD), lambda b, c: (c, 0))
    st = pl.BlockSpec((None, RET_H, RET_DK, RET_DV), lambda b, c: (b, 0, 0, 0))
    full = lambda shape: pl.BlockSpec(shape, lambda b, c: (0,) * len(shape))
    y, s_last = pl.pallas_call(
        _retention_kernel,
        grid=(batch, nc),
        in_specs=[zq, zk, zv, zg, tab, tab, st, full(intra.shape), full(qdec.shape), full(kdec.shape),
                  full(cdec.shape), full((1, RET_W))],
        out_specs=[pl.BlockSpec((chunk, RET_W), lambda b, c: (row(b, c), 0)), st],
        out_shape=[jax.ShapeDtypeStruct((batch * seq, RET_W), F32),
                   jax.ShapeDtypeStruct((batch, RET_H, RET_DK, RET_DV), F32)],
        scratch_shapes=[pltpu.VMEM((RET_H, RET_DK, RET_DV), F32)],
        compiler_params=_params(("arbitrary", "arbitrary")),
        name="retention",
    )(z, z, z, z, cos, sin, s0, intra, qdec, kdec, cdec, gn.reshape(1, RET_W))
    return y, s_last


def _retention_consts(seq):
    chunk = min(RET_CHUNK, seq)
    log_g = jnp.log1p(-jnp.exp2(-5.0 - jnp.arange(RET_H, dtype=F32)))
    idx = jnp.arange(chunk, dtype=F32)
    diff = idx[:, None] - idx[None, :]
    intra = jnp.where(diff >= 0, jnp.exp(log_g[:, None, None] * jnp.maximum(diff, 0.0)), 0.0)
    q_dec = jnp.exp(log_g[:, None] * (idx + 1.0))
    k_dec = jnp.exp(log_g[:, None] * (chunk - 1.0 - idx))
    chunk_dec = jnp.exp(log_g * chunk)
    expand = lambda t: jnp.pad(jnp.repeat(t.T, RET_DK, axis=1), ((0, 0), (0, RET_QK_PAD - RET_H * RET_DK)))
    cdec = jnp.broadcast_to(chunk_dec[:, None, None], (RET_H, RET_DK, RET_DV))
    return intra, expand(q_dec), expand(k_dec), cdec


def _mla_common(qa_ref, kva_ref, kra_ref, qg_ref, kvg_ref, wuq_ref, cq_ref, sq_ref, ckr_ref, skr_ref):
    cq = _rms(qa_ref[...], qg_ref[...]).astype(BF16)
    qh = _dot(cq, wuq_ref[...])
    cos = cq_ref[...]
    sin = sq_ref[...]
    scale = (MLA_NOPE + MLA_ROPE) ** -0.5
    q_heads = [_rope(qh[:, h * HEAD_PAD:(h + 1) * HEAD_PAD], cos, sin) * scale for h in range(MLA_H)]
    c_new = _rms(kva_ref[...], kvg_ref[...])
    kr_new = _rope(kra_ref[...], ckr_ref[...], skr_ref[...])
    return q_heads, c_new, kr_new


def _mla_prep_prompt_kernel(qa_ref, kva_ref, kra_ref, qg_ref, kvg_ref, wuq_ref, cq_ref, sq_ref, ckr_ref,
                            skr_ref, wk_ref, wvt_ref, c_ref, kr_ref, q_ref, k_ref, vt_ref):
    q_heads, c_new, kr_new = _mla_common(qa_ref, kva_ref, kra_ref, qg_ref, kvg_ref, wuq_ref, cq_ref, sq_ref,
                                         ckr_ref, skr_ref)
    c_ref[...] = c_new
    kr_ref[...] = kr_new[:, :MLA_ROPE]
    for h in range(MLA_H):
        q_ref[:, h * HEAD_PAD:(h + 1) * HEAD_PAD] = (q_heads[h] * LOG2_E).astype(BF16)
    ckr = jnp.concatenate([c_new, kr_new], axis=1).astype(BF16)
    k_ref[...] = _dot(ckr, wk_ref[...]).astype(BF16)
    vt_ref[...] = _dot_nt(wvt_ref[...], ckr[:, :MLA_KV_LORA]).astype(BF16)


def _mla_prep_sample_kernel(qa_ref, kva_ref, kra_ref, qg_ref, kvg_ref, wuq_ref, cq_ref, sq_ref, ckr_ref,
                            skr_ref, wabs_ref, c_ref, kr_ref, q_ref):
    q_heads, c_new, kr_new = _mla_common(qa_ref, kva_ref, kra_ref, qg_ref, kvg_ref, wuq_ref, cq_ref, sq_ref,
                                         ckr_ref, skr_ref)
    c_ref[...] = c_new
    kr_ref[...] = kr_new[:, :MLA_ROPE]
    for h in range(MLA_H):
        q_ref[:, h * QABS_W:(h + 1) * QABS_W] = _dot(q_heads[h].astype(BF16), wabs_ref[h]).astype(BF16)


def _mla_prep(z, tabs, qg, kvg, wuq_p, extra, prompt):
    rows = z.shape[0]
    tm = min(512, rows)
    cq, sq, ckr, skr = tabs
    zc = lambda c, w: pl.BlockSpec((tm, w), lambda i, c=c, w=w: (i, c // w))
    tab = pl.BlockSpec((tm, LANES), lambda i: (i, 0))
    full = lambda shape: pl.BlockSpec(shape, lambda i: (0,) * len(shape))
    in_specs = [zc(Z_QA, MLA_Q_LORA), zc(Z_KVA, MLA_KV_LORA), zc(Z_KRA, LANES), full((1, MLA_Q_LORA)),
                full((1, MLA_KV_LORA)), full(wuq_p.shape), tab, tab, tab, tab] + [full(e.shape) for e in extra]
    rowblk = lambda w: pl.BlockSpec((tm, w), lambda i: (i, 0))
    out_specs = [rowblk(MLA_KV_LORA), rowblk(MLA_ROPE)]
    out_shape = [jax.ShapeDtypeStruct((rows, MLA_KV_LORA), F32), jax.ShapeDtypeStruct((rows, MLA_ROPE), F32)]
    if prompt:
        widths = [MLA_H * HEAD_PAD] * 2
        body = _mla_prep_prompt_kernel
    else:
        widths = [MLA_H * QABS_W]
        body = _mla_prep_sample_kernel
    out_specs += [rowblk(w) for w in widths]
    out_shape += [jax.ShapeDtypeStruct((rows, w), BF16) for w in widths]
    if prompt:
        out_specs.append(pl.BlockSpec((MLA_W, tm), lambda i: (0, i)))
        out_shape.append(jax.ShapeDtypeStruct((MLA_W, rows), BF16))
    return pl.pallas_call(
        body,
        grid=(rows // tm,),
        in_specs=in_specs,
        out_specs=out_specs,
        out_shape=out_shape,
        compiler_params=_params(("parallel",)),
        name="mla_prep_prompt" if prompt else "mla_prep_sample",
    )(z, z, z, qg.reshape(1, -1), kvg.reshape(1, -1), wuq_p, cq, sq, ckr, skr, *extra)


def _flash_kernel(qi_ref, ki_ref, q_ref, k_ref, vt_ref, o_ref, m_ref, l_ref, acc_ref):
    qi = qi_ref[pl.program_id(0)]
    ki = ki_ref[pl.program_id(0)]
    tq = q_ref.shape[0]
    tk = k_ref.shape[0]

    @pl.when(ki == 0)
    def _():
        m_ref[...] = jnp.full_like(m_ref, -jnp.inf)
        l_ref[...] = jnp.zeros_like(l_ref)
        acc_ref[...] = jnp.zeros_like(acc_ref)

    def update(masked):
        if masked:
            keep = (lax.broadcasted_iota(jnp.int32, (tk, tq), 0) <= lax.broadcasted_iota(jnp.int32, (tk, tq), 1))
        def scores(h):
            hs = slice(h * HEAD_PAD, (h + 1) * HEAD_PAD)
            return _dot_nt(k_ref[:, hs], q_ref[:, hs])

        s_next = scores(0)
        for h in range(MLA_H):
            s = s_next
            if h + 1 < MLA_H:
                s_next = scores(h + 1)
            if masked:
                s = jnp.where(keep, s, -jnp.inf)
            m_old = m_ref[h]
            m_new = jnp.maximum(m_old, jnp.max(s, axis=0, keepdims=True))
            p = jnp.exp2(s - m_new)
            alpha = jnp.exp2(m_old - m_new)
            l_ref[h] = alpha * l_ref[h] + jnp.sum(p, axis=0, keepdims=True)
            pv = _dot(vt_ref[h * MLA_DV:(h + 1) * MLA_DV, :], p.astype(BF16))
            acc_ref[h] = alpha * acc_ref[h] + pv
            m_ref[h] = m_new

    @pl.when(ki < qi)
    def _():
        update(False)

    @pl.when(ki == qi)
    def _():
        update(True)
        for h in range(MLA_H):
            o_ref[:, h * MLA_DV:(h + 1) * MLA_DV] = (acc_ref[h] / l_ref[h]).T


def _flash(q, k, vt):
    rows = q.shape[0]
    t = min(512, rows)
    n = rows // t
    w = MLA_H * HEAD_PAD
    pairs = [(i, j) for i in range(n) for j in range(i + 1)]
    qi_tab = jnp.asarray([p[0] for p in pairs], jnp.int32)
    ki_tab = jnp.asarray([p[1] for p in pairs], jnp.int32)
    grid_spec = pltpu.PrefetchScalarGridSpec(
        num_scalar_prefetch=2,
        grid=(len(pairs),),
        in_specs=[pl.BlockSpec((t, w), lambda s, qi, ki: (qi[s], 0)),
                  pl.BlockSpec((t, w), lambda s, qi, ki: (ki[s], 0)),
                  pl.BlockSpec((MLA_W, t), lambda s, qi, ki: (0, ki[s]))],
        out_specs=pl.BlockSpec((t, MLA_W), lambda s, qi, ki: (qi[s], 0)),
        scratch_shapes=[pltpu.VMEM((MLA_H, 1, t), F32), pltpu.VMEM((MLA_H, 1, t), F32),
                        pltpu.VMEM((MLA_H, MLA_DV, t), F32)],
    )
    return pl.pallas_call(
        _flash_kernel,
        grid_spec=grid_spec,
        out_shape=jax.ShapeDtypeStruct((rows, MLA_W), F32),
        compiler_params=_params(("arbitrary",)),
        name="flash_prompt",
    )(qi_tab, ki_tab, q, k, vt)


def _paged_kernel(pages, group, pt_ref, q_ref, cn_ref, krn_ref, *refs):
    lat_refs = refs[:pages]
    krt_refs = refs[pages:2 * pages]
    o_ref, m_ref, l_ref, acc_ref = refs[2 * pages:]
    ji = pl.program_id(1)

    @pl.when(ji == 0)
    def _():
        m_ref[...] = jnp.full_like(m_ref, -jnp.inf)
        l_ref[...] = jnp.zeros_like(l_ref)
        acc_ref[...] = jnp.zeros_like(acc_ref)

    q_lat = q_ref[:, :MLA_KV_LORA]
    q_pe = q_ref[:, MLA_KV_LORA:MLA_KV_LORA + MLA_ROPE]

    def local_softmax(s, values):
        m = jnp.max(s, axis=1, keepdims=True)
        p = jnp.exp(s - m)
        return m, jnp.sum(p, axis=1, keepdims=True), _dot(p.astype(BF16), values)

    def merge(parts):
        m_old = m_ref[...]
        m_new = m_old
        for m, _, _ in parts:
            m_new = jnp.maximum(m_new, m)
        alpha = jnp.exp(m_old - m_new)
        l_new = alpha * l_ref[...]
        acc = alpha * acc_ref[...]
        for m, l, o in parts:
            w = jnp.exp(m - m_new)
            l_new = l_new + w * l
            acc = acc + w * o
        m_ref[...] = m_new
        l_ref[...] = l_new
        acc_ref[...] = acc

    values, scores = [], []
    for g in range(pages // group):
        ids = range(g * group, (g + 1) * group)
        c = jnp.concatenate([lat_refs[i][...].astype(BF16) for i in ids], axis=0)
        krt = jnp.concatenate([krt_refs[i][...].astype(BF16) for i in ids], axis=1)
        values.append(c)
        scores.append(_dot_nt(q_lat, c) + _dot(q_pe, krt))
    merge([local_softmax(s, c) for s, c in zip(scores, values)])

    @pl.when(ji == pl.num_programs(1) - 1)
    def _():
        c_new = cn_ref[...].astype(BF16)
        s = _dot_nt(q_lat, c_new) + _dot_nt(q_pe, krn_ref[...].astype(BF16))
        t_q = lax.broadcasted_iota(jnp.int32, s.shape, 0) % SUBLANES
        t_k = lax.broadcasted_iota(jnp.int32, s.shape, 1)
        merge([local_softmax(jnp.where(t_k <= t_q, s, -jnp.inf), c_new)])
        o_ref[...] = acc_ref[...] / l_ref[...]


def _paged_attention(layer, q, c_new, kr_new, cache_lat, cache_krt, page_table):
    batch, rows, _ = q.shape
    t_new = c_new.shape[1]
    n_pages = page_table.shape[1]
    page = cache_lat.shape[2]
    pages = min(16, n_pages)
    group = min(4, pages)
    nch = n_pages // pages

    def page_spec(shape, i):
        return pl.BlockSpec((None, None) + shape,
                            lambda b, j, pt, i=i: (layer, pt[b * n_pages + j * pages + i], 0, 0))

    per_b = lambda r, w: pl.BlockSpec((None, r, w), lambda b, j, pt: (b, 0, 0))
    grid_spec = pltpu.PrefetchScalarGridSpec(
        num_scalar_prefetch=1,
        grid=(batch, nch),
        in_specs=[per_b(rows, QABS_W), per_b(t_new, MLA_KV_LORA), per_b(t_new, MLA_ROPE)]
        + [page_spec((page, MLA_KV_LORA), i) for i in range(pages)]
        + [page_spec((MLA_ROPE, page), i) for i in range(pages)],
        out_specs=per_b(rows, MLA_KV_LORA),
        scratch_shapes=[pltpu.VMEM((rows, 1), F32), pltpu.VMEM((rows, 1), F32),
                        pltpu.VMEM((rows, MLA_KV_LORA), F32)],
    )
    return pl.pallas_call(
        functools.partial(_paged_kernel, pages, group),
        grid_spec=grid_spec,
        out_shape=jax.ShapeDtypeStruct((batch, rows, MLA_KV_LORA), F32),
        compiler_params=_params(("parallel", "arbitrary")),
        name="paged_sample",
    )(page_table.reshape(-1), q, c_new, kr_new, *([cache_lat] * pages), *([cache_krt] * pages))


def _uv_kernel(o_ref, w_ref, y_ref):
    nb = o_ref.shape[0]
    t = o_ref.shape[2]
    for h in range(MLA_H):
        o_h = o_ref[:, h].reshape(nb * t, MLA_KV_LORA).astype(BF16)
        y_ref[:, h * MLA_DV:(h + 1) * MLA_DV] = _dot(o_h, w_ref[h])


def _uv_proj(o_lat, w_uv_h):
    batch, _, t, _ = o_lat.shape
    return pl.pallas_call(
        _uv_kernel,
        grid=(1,),
        in_specs=[pl.BlockSpec(o_lat.shape, lambda i: (0, 0, 0, 0)),
                  pl.BlockSpec(w_uv_h.shape, lambda i: (0, 0, 0))],
        out_specs=pl.BlockSpec((batch * t, MLA_W), lambda i: (0, 0)),
        out_shape=jax.ShapeDtypeStruct((batch * t, MLA_W), F32),
        compiler_params=_params(("arbitrary",)),
        name="uv_proj",
    )(o_lat, w_uv_h)


def _out_proj_kernel(final, x_ref, ya_ref, yb_ref, oc_ref, gm_ref, w_ref, fg_ref, *out_refs):
    yc = oc_ref[...] * _silu(gm_ref[...])
    upd = _dot(ya_ref[...].astype(BF16), w_ref[0:LRU_W, :])
    upd = upd + _dot(yb_ref[...].astype(BF16), w_ref[LRU_W:LRU_W + RET_W, :])
    upd = upd + _dot(yc.astype(BF16), w_ref[LRU_W + RET_W:, :])
    x_new = x_ref[...] + upd
    out_refs[0][...] = x_new
    if final:
        out_refs[1][...] = _rms(x_new, fg_ref[...])


def _out_proj(x2d, y_a, y_b, o_c, z, w_out, final_g, final):
    rows = x2d.shape[0]
    tm = min(512, rows)
    rowblk = lambda w: pl.BlockSpec((tm, w), lambda i: (i, 0))
    full = lambda shape: pl.BlockSpec(shape, lambda i: (0,) * len(shape))
    n_out = 2 if final else 1
    outs = pl.pallas_call(
        functools.partial(_out_proj_kernel, final),
        grid=(rows // tm,),
        in_specs=[rowblk(D_MODEL), rowblk(LRU_W), rowblk(RET_W), rowblk(MLA_W),
                  pl.BlockSpec((tm, MLA_W), lambda i: (i, Z_GM // MLA_W)), full(w_out.shape), full((1, D_MODEL))],
        out_specs=[rowblk(D_MODEL)] * n_out,
        out_shape=[jax.ShapeDtypeStruct((rows, D_MODEL), F32)] * n_out,
        compiler_params=_params(("parallel",)),
        name="out_proj_final" if final else "out_proj",
    )(x2d, y_a, y_b, o_c, z, w_out, final_g.reshape(1, D_MODEL))
    return outs


def _pad_cols(w, width):
    return jnp.pad(w, ((0, 0), (0, width - w.shape[1])))


def _layout_w_in(w):
    splits = np.cumsum([0, LRU_W, LRU_W, RET_H * RET_DK, RET_H * RET_DK, RET_W, RET_W, MLA_Q_LORA, MLA_KV_LORA,
                        MLA_ROPE, MLA_W])
    seg = [w[:, splits[i]:splits[i + 1]] for i in range(10)]
    x_lru, g_lru, q_r, k_r, v_r, g_r, q_a, kv_a, kr_a, g_m = seg
    cols = [x_lru, g_lru, _pad_cols(q_r, RET_QK_PAD), _pad_cols(k_r, RET_QK_PAD), q_a, kv_a, v_r, g_r, g_m,
            _pad_cols(kr_a, LANES)]
    return jnp.concatenate(cols, axis=1).astype(BF16)


def _block_diag(w):
    n, c, d = w.shape
    eye = jnp.eye(n, dtype=w.dtype)
    return (eye[:, None, :, None] * w[:, :, None, :]).reshape(n * c, n * d)


def _layout_w_uq(w):
    per_head = w.reshape(MLA_Q_LORA, MLA_H, MLA_NOPE + MLA_ROPE)
    per_head = jnp.pad(per_head, ((0, 0), (0, 0), (0, HEAD_PAD - MLA_NOPE - MLA_ROPE)))
    return per_head.reshape(MLA_Q_LORA, MLA_H * HEAD_PAD).astype(BF16)


def _layout_w_key(w_uk):
    top = jnp.pad(w_uk, ((0, 0), (0, 0), (0, HEAD_PAD - MLA_NOPE)))
    rope_rows = jnp.zeros((LANES, MLA_H, HEAD_PAD), F32)
    eye = jnp.eye(MLA_ROPE, dtype=F32)
    rope_rows = rope_rows.at[:MLA_ROPE, :, MLA_NOPE:MLA_NOPE + MLA_ROPE].set(
        jnp.broadcast_to(eye[:, None, :], (MLA_ROPE, MLA_H, MLA_ROPE)))
    return jnp.concatenate([top, rope_rows], axis=0).reshape(MLA_KV_LORA + LANES, MLA_H * HEAD_PAD).astype(BF16)


def _layout_w_val_t(w_uv):
    return w_uv.reshape(MLA_KV_LORA, MLA_W).T.astype(BF16)


def _layout_w_abs(w_uk):
    w = jnp.zeros((MLA_H, HEAD_PAD, QABS_W), F32)
    w = w.at[:, :MLA_NOPE, :MLA_KV_LORA].set(w_uk.transpose(1, 2, 0))
    w = w.at[:, MLA_NOPE:MLA_NOPE + MLA_ROPE, MLA_KV_LORA:MLA_KV_LORA + MLA_ROPE].set(
        jnp.broadcast_to(jnp.eye(MLA_ROPE, dtype=F32), (MLA_H, MLA_ROPE, MLA_ROPE)))
    return w.astype(BF16)


def _rope_tables(pos):
    half = MLA_ROPE // 2
    inv = ROPE_BASE ** (-jnp.arange(half, dtype=F32) / half)
    ang = pos[:, None] * inv[None, :]
    cos = jnp.cos(ang)
    sin = jnp.sin(ang)
    c32 = jnp.concatenate([cos, cos], axis=1)
    s32 = jnp.concatenate([-sin, sin], axis=1)
    t = pos.shape[0]
    ret_c = _pad_cols(jnp.tile(c32, (1, RET_H)), RET_QK_PAD)
    ret_s = _pad_cols(jnp.tile(s32, (1, RET_H)), RET_QK_PAD)
    q_c = jnp.concatenate([jnp.ones((t, MLA_NOPE), F32), c32, jnp.zeros((t, HEAD_PAD - MLA_NOPE - MLA_ROPE), F32)], axis=1)
    q_s = jnp.concatenate([jnp.zeros((t, MLA_NOPE), F32), s32, jnp.zeros((t, HEAD_PAD - MLA_NOPE - MLA_ROPE), F32)], axis=1)
    kr_c = _pad_cols(c32, LANES)
    kr_s = _pad_cols(s32, LANES)
    return (ret_c, ret_s), (q_c, q_s, kr_c, kr_s)


def kernel(x_prompt, x_sample, cache_mla_latent, cache_mla_krope, state_ret, state_lru_h, state_conv, page_table,
           norm_g, w_in, conv_w, conv_b, lru_wa, lru_ba, lru_wx, lru_bx, lru_lambda, ret_gn_g, q_norm_g, w_uq,
           kv_norm_g, w_uk, w_uv, w_out, final_norm_g):
    bp, tp, _ = x_prompt.shape
    bs, ts, _ = x_sample.shape
    depth = w_in.shape[0]
    past_len = page_table.shape[1] * cache_mla_latent.shape[2]
    assert bp == 1 and ts == SUBLANES

    pos_p = jnp.arange(tp, dtype=F32)
    pos_s = past_len + jnp.arange(ts, dtype=F32)
    ret_tab_p, mla_tab_p = _rope_tables(pos_p)
    ret_tab_s, mla_tab_s = _rope_tables(pos_s)
    ret_const_p = _retention_consts(tp)
    ret_const_s = _retention_consts(ts)

    cache_krt = jnp.swapaxes(cache_mla_krope, 2, 3)

    xp = x_prompt.reshape(bp * tp, D_MODEL)
    xs = x_sample.reshape(bs * ts, D_MODEL)
    yp = ys = None
    outs = [[] for _ in range(10)]
    for l in range(depth):
        final = l == depth - 1
        w_in_p = _layout_w_in(w_in[l])
        lw = (conv_w[l], conv_b[l].reshape(1, -1), _block_diag(lru_wa[l]).astype(BF16), lru_ba[l].reshape(1, -1),
              _block_diag(lru_wx[l]).astype(BF16), lru_bx[l].reshape(1, -1), lru_lambda[l].reshape(1, -1))
        wuq_p = _layout_w_uq(w_uq[l])
        w_out_b = w_out[l].astype(BF16)

        z = _in_proj(xp, norm_g[l], w_in_p)
        y_a, h_last = _lru_prompt(z, lw)
        y_b, s_last = _retention(z, *ret_tab_p, jnp.zeros((bp, RET_H, RET_DK, RET_DV), F32), ret_const_p,
                                 ret_gn_g[l], bp, tp)
        c_new, kr_new, q, k, vt = _mla_prep(z, mla_tab_p, q_norm_g[l], kv_norm_g[l], wuq_p,
                                            (_layout_w_key(w_uk[l]), _layout_w_val_t(w_uv[l])), True)
        o_c = _flash(q, k, vt)
        res = _out_proj(xp, y_a, y_b, o_c, z, w_out_b, final_norm_g, final)
        xp = res[0]
        if final:
            yp = res[1]
        conv_rows = z[:, Z_XLRU:Z_XLRU + LRU_W].reshape(bp, tp, LRU_W)[:, tp - (CONV_K - 1):, :]
        for lst, val in zip(outs[:5], (c_new.reshape(bp, tp, -1), kr_new.reshape(bp, tp, -1), s_last,
                                       h_last.reshape(bp, LRU_W), conv_rows)):
            lst.append(val)

        z = _in_proj(xs, norm_g[l], w_in_p)
        buf = jnp.pad(state_conv[l], ((0, 0), (SUBLANES - (CONV_K - 1), 0), (0, 0)))
        xprev = jnp.roll(buf, -1, axis=0).reshape(bs * ts, LRU_W)
        y_a, h_last = _lru_sample(z, xprev, state_lru_h[l], lw)
        y_b, s_last = _retention(z, *ret_tab_s, state_ret[l], ret_const_s, ret_gn_g[l], bs, ts)
        tabs_s = tuple(jnp.tile(t, (bs, 1)) for t in mla_tab_s)
        c_new, kr_new, q_abs = _mla_prep(z, tabs_s, q_norm_g[l], kv_norm_g[l], wuq_p, (_layout_w_abs(w_uk[l]),), False)
        q_abs = q_abs.reshape(bs, ts, MLA_H, QABS_W).transpose(0, 2, 1, 3).reshape(bs, MLA_H * ts, QABS_W)
        o_lat = _paged_attention(l, q_abs, c_new.reshape(bs, ts, -1), kr_new.reshape(bs, ts, -1),
                                 cache_mla_latent, cache_krt, page_table)
        o_c = _uv_proj(o_lat.reshape(bs, MLA_H, ts, MLA_KV_LORA), w_uv[l].transpose(1, 0, 2).astype(BF16))
        res = _out_proj(xs, y_a, y_b, o_c, z, w_out_b, final_norm_g, final)
        xs = res[0]
        if final:
            ys = res[1]
        conv_rows = z[:, Z_XLRU:Z_XLRU + LRU_W].reshape(bs, ts, LRU_W)[:, ts - (CONV_K - 1):, :]
        for lst, val in zip(outs[5:], (c_new.reshape(bs, ts, -1), kr_new.reshape(bs, ts, -1), s_last, h_last,
                                       conv_rows)):
            lst.append(val)

    return (yp.reshape(bp, tp, D_MODEL), ys.reshape(bs, ts, D_MODEL)) + tuple(jnp.stack(o) for o in outs)
```

```python
import functools

import jax
import jax.numpy as jnp
import numpy as np
from jax import lax
from jax.experimental import pallas as pl
from jax.experimental.pallas import tpu as pltpu

F32 = jnp.float32
BF16 = jnp.bfloat16

D_MODEL = 1024
LRU_W = 256
LRU_BLOCKS = 4
CONV_K = 4
LRU_C = 8.0
RET_H = 6
RET_DK = 32
RET_DV = 64
RET_W = RET_H * RET_DV
RET_CHUNK = 128
MLA_H = 6
MLA_NOPE = 64
MLA_ROPE = 32
MLA_DV = 64
MLA_W = MLA_H * MLA_DV
MLA_Q_LORA = 256
MLA_KV_LORA = 256
ROPE_BASE = 10000.0
NORM_EPS = 1e-6
LOG2_E = 1.4426950408889634

LANES = 128
SUBLANES = 8
HEAD_PAD = 128
BF16_ROWS = 16
VT_ROWS = MLA_DV + BF16_ROWS
QABS_W = 384
FLASH_STRIP = 512
FLASH_AHEAD = 2
VMEM_LIMIT = 48 * 1024 * 1024

Z_XLRU, Z_GLRU, Z_QR, Z_KR, Z_QA, Z_KVA = 0, 256, 512, 768, 1024, 1280
Z_VR, Z_GR, Z_GM, Z_KRA = 1536, 1920, 2304, 2688
Z_W = 2816
RET_QK_PAD = 256


def _params(sem):
    return pltpu.CompilerParams(dimension_semantics=sem, vmem_limit_bytes=VMEM_LIMIT)


def _rms(x, g):
    return x * lax.rsqrt(jnp.mean(x * x, axis=-1, keepdims=True) + NORM_EPS) * g


def _silu(x):
    return x * jax.nn.sigmoid(x)


def _rope(x, c, s):
    n = x.shape[-1]
    lane = lax.broadcasted_iota(jnp.int32, x.shape, x.ndim - 1)
    up = pltpu.roll(x, n - MLA_ROPE // 2, axis=x.ndim - 1)
    dn = pltpu.roll(x, MLA_ROPE // 2, axis=x.ndim - 1)
    partner = jnp.where((lane % MLA_ROPE) < MLA_ROPE // 2, up, dn)
    return x * c + partner * s


def _dot(a, b):
    return jnp.dot(a, b, preferred_element_type=F32)


def _dot_nt(a, b):
    return lax.dot_general(a, b, (((1,), (1,)), ((), ())), preferred_element_type=F32)


def _dot_tn(a, b):
    return lax.dot_general(a, b, (((0,), (0,)), ((), ())), preferred_element_type=F32)


def _in_proj_kernel(x_ref, g_ref, w_ref, z_ref):
    h = _rms(x_ref[...], g_ref[...])
    z_ref[...] = _dot(h.astype(BF16), w_ref[...])


def _in_proj(x2d, g, w_p):
    rows = x2d.shape[0]
    tm = min(512, rows)
    return pl.pallas_call(
        _in_proj_kernel,
        grid=(rows // tm,),
        in_specs=[
            pl.BlockSpec((tm, D_MODEL), lambda i: (i, 0)),
            pl.BlockSpec((1, D_MODEL), lambda i: (0, 0)),
            pl.BlockSpec((D_MODEL, Z_W), lambda i: (0, 0)),
        ],
        out_specs=pl.BlockSpec((tm, Z_W), lambda i: (i, 0)),
        out_shape=jax.ShapeDtypeStruct((rows, Z_W), F32),
        compiler_params=_params(("parallel",)),
        name="in_proj",
    )(x2d, g.reshape(1, D_MODEL), w_p)


def _group_scan(a, u):
    row = lax.broadcasted_iota(jnp.int32, a.shape, 0) % SUBLANES
    d = 1
    while d < SUBLANES:
        keep = row >= d
        a_prev = jnp.where(keep, pltpu.roll(a, d, axis=0), 1.0)
        u_prev = jnp.where(keep, pltpu.roll(u, d, axis=0), 0.0)
        u = a * u_prev + u
        a = a * a_prev
        d *= 2
    return a, u


def _lru_gates(shifted, cw_ref, cb_ref, wa_ref, ba_ref, wx_ref, bx_ref, lam_ref):
    xc = cb_ref[...] + shifted[3] * cw_ref[0:1, :]
    xc = xc + shifted[2] * cw_ref[1:2, :]
    xc = xc + shifted[1] * cw_ref[2:3, :]
    xc = xc + shifted[0] * cw_ref[3:4, :]
    xcb = xc.astype(BF16)
    gate_r = jax.nn.sigmoid(_dot(xcb, wa_ref[...]) + ba_ref[...])
    gate_i = jax.nn.sigmoid(_dot(xcb, wx_ref[...]) + bx_ref[...])
    neg_lam = -lam_ref[...]
    softplus = jnp.maximum(neg_lam, 0.0) + jnp.log1p(jnp.exp(-jnp.abs(neg_lam)))
    log_a = -LRU_C * gate_r * softplus
    a = jnp.exp(log_a)
    u = jnp.sqrt(-jnp.tanh(log_a) * (a * a + 1.0)) * (gate_i * xc)
    return a, u


def _lru_prompt_kernel(x_ref, g_ref, cw_ref, cb_ref, wa_ref, ba_ref, wx_ref, bx_ref, lam_ref,
                       y_ref, hl_ref, xbuf_ref, a_ref, u_ref, h_ref):
    tm = x_ref.shape[0]

    @pl.when(pl.program_id(0) == 0)
    def _():
        xbuf_ref[0:SUBLANES, :] = jnp.zeros((SUBLANES, LRU_W), F32)
        h_ref[...] = jnp.zeros_like(h_ref)

    xbuf_ref[SUBLANES:, :] = x_ref[...]
    xb = xbuf_ref[...]
    shifted = [x_ref[...]] + [pltpu.roll(xb, d, axis=0)[SUBLANES:, :] for d in range(1, CONV_K)]
    xbuf_ref[0:SUBLANES, :] = x_ref[tm - SUBLANES:, :]

    a, u = _lru_gates(shifted, cw_ref, cb_ref, wa_ref, ba_ref, wx_ref, bx_ref, lam_ref)
    a, u = _group_scan(a, u)
    a_ref[...] = a
    u_ref[...] = u

    def body(g, h_prev):
        r0 = pl.multiple_of(g * SUBLANES, SUBLANES)
        h = u_ref[pl.ds(r0, SUBLANES), :] + a_ref[pl.ds(r0, SUBLANES), :] * h_prev
        u_ref[pl.ds(r0, SUBLANES), :] = h
        return jnp.broadcast_to(h[SUBLANES - 1:SUBLANES, :], (SUBLANES, LRU_W))

    h_last = lax.fori_loop(0, tm // SUBLANES, body, h_ref[...])
    h_ref[...] = h_last
    hl_ref[...] = h_last[0:1, :]
    y_ref[...] = u_ref[...] * _silu(g_ref[...])


def _lru_sample_kernel(x_ref, xprev_ref, g_ref, h0_ref, cw_ref, cb_ref, wa_ref, ba_ref, wx_ref, bx_ref,
                       lam_ref, y_ref, hs_ref):
    rows = x_ref.shape[0]
    x = x_ref[...]
    xprev = xprev_ref[...]
    row = lax.broadcasted_iota(jnp.int32, x.shape, 0) % SUBLANES
    shifted = [x] + [jnp.where(row >= d, pltpu.roll(x, d, axis=0), pltpu.roll(xprev, d, axis=0))
                     for d in range(1, CONV_K)]
    a, u = _lru_gates(shifted, cw_ref, cb_ref, wa_ref, ba_ref, wx_ref, bx_ref, lam_ref)
    a, u = _group_scan(a, u)
    nb = rows // SUBLANES
    h = u.reshape(nb, SUBLANES, LRU_W) + a.reshape(nb, SUBLANES, LRU_W) * h0_ref[...][:, None, :]
    h = h.reshape(rows, LRU_W)
    hs_ref[...] = h
    y_ref[...] = h * _silu(g_ref[...])


def _lru_weight_specs():
    full = lambda shape: pl.BlockSpec(shape, lambda *_: (0,) * len(shape))
    return [full((CONV_K, LRU_W)), full((1, LRU_W)), full((LRU_W, LRU_W)), full((1, LRU_W)),
            full((LRU_W, LRU_W)), full((1, LRU_W)), full((1, LRU_W))]


def _lru_prompt(z, lw):
    rows = z.shape[0]
    tm = min(512, rows)
    col = lambda c: pl.BlockSpec((tm, LRU_W), lambda i, c=c: (i, c // LRU_W))
    y, h_last = pl.pallas_call(
        _lru_prompt_kernel,
        grid=(rows // tm,),
        in_specs=[col(Z_XLRU), col(Z_GLRU)] + _lru_weight_specs(),
        out_specs=[pl.BlockSpec((tm, LRU_W), lambda i: (i, 0)), pl.BlockSpec((1, LRU_W), lambda i: (0, 0))],
        out_shape=[jax.ShapeDtypeStruct((rows, LRU_W), F32), jax.ShapeDtypeStruct((1, LRU_W), F32)],
        scratch_shapes=[pltpu.VMEM((tm + SUBLANES, LRU_W), F32), pltpu.VMEM((tm, LRU_W), F32),
                        pltpu.VMEM((tm, LRU_W), F32), pltpu.VMEM((SUBLANES, LRU_W), F32)],
        compiler_params=_params(("arbitrary",)),
        name="lru_prompt",
    )(z, z, *lw)
    return y, h_last


def _lru_sample(z, xprev, h0, lw):
    rows = z.shape[0]
    nb = rows // SUBLANES
    col = lambda c: pl.BlockSpec((rows, LRU_W), lambda i, c=c: (0, c // LRU_W))
    y, hs = pl.pallas_call(
        _lru_sample_kernel,
        grid=(1,),
        in_specs=[col(Z_XLRU), pl.BlockSpec((rows, LRU_W), lambda i: (0, 0)), col(Z_GLRU),
                  pl.BlockSpec((nb, LRU_W), lambda i: (0, 0))] + _lru_weight_specs(),
        out_specs=[pl.BlockSpec((rows, LRU_W), lambda i: (0, 0))] * 2,
        out_shape=[jax.ShapeDtypeStruct((rows, LRU_W), F32)] * 2,
        compiler_params=_params(("arbitrary",)),
        name="lru_sample",
    )(z, xprev, z, h0, *lw)
    return y, hs.reshape(nb, SUBLANES, LRU_W)[:, SUBLANES - 1, :]


def _retention_kernel(q_ref, k_ref, v_ref, g_ref, c_ref, s_ref, s0_ref, intra_ref, qdec_ref, kdec_ref,
                      cdec_ref, gn_ref, y_ref, sl_ref, st_ref):
    ci = pl.program_id(1)

    @pl.when(ci == 0)
    def _():
        st_ref[...] = s0_ref[...]

    cos = c_ref[...]
    sin = s_ref[...]
    q = _rope(q_ref[...], cos, sin)
    k = _rope(k_ref[...], cos, sin) * (RET_DK ** -0.5)
    v = v_ref[...]
    gate = _silu(g_ref[...])
    qd = q * qdec_ref[...]
    kd = k * kdec_ref[...]
    heads = range(RET_H)
    ks = [slice(h * RET_DK, (h + 1) * RET_DK) for h in heads]
    vs = [slice(h * RET_DV, (h + 1) * RET_DV) for h in heads]
    vh = [v[:, vs[h]].astype(BF16) for h in heads]
    scores = [_dot_nt(q[:, ks[h]].astype(BF16), k[:, ks[h]].astype(BF16)) for h in heads]
    states = [st_ref[h] for h in heads]
    cross = [_dot(qd[:, ks[h]].astype(BF16), states[h].astype(BF16)) for h in heads]
    for h in heads:
        st_ref[h] = cdec_ref[h] * states[h] + _dot_tn(kd[:, ks[h]].astype(BF16), vh[h])
    for h in heads:
        o = _dot((scores[h] * intra_ref[h]).astype(BF16), vh[h]) + cross[h]
        o = o * lax.rsqrt(jnp.mean(o * o, axis=-1, keepdims=True) + NORM_EPS) * gn_ref[:, vs[h]]
        y_ref[:, vs[h]] = o * gate[:, vs[h]]

    @pl.when(ci == pl.num_programs(1) - 1)
    def _():
        sl_ref[...] = st_ref[...]


def _retention(z, cos, sin, s0, consts, gn, batch, seq):
    intra, qdec, kdec, cdec = consts
    chunk = intra.shape[-1]
    nc = seq // chunk
    row = lambda b, c: b * nc + c
    zq = pl.BlockSpec((chunk, RET_QK_PAD), lambda b, c: (row(b, c), Z_QR // RET_QK_PAD))
    zk = pl.BlockSpec((chunk, RET_QK_PAD), lambda b, c: (row(b, c), Z_KR // RET_QK_PAD))
    zv = pl.BlockSpec((chunk, RET_W), lambda b, c: (row(b, c), Z_VR // RET_W))
    zg = pl.BlockSpec((chunk, RET_W), lambda b, c: (row(b, c), Z_GR // RET_W))
    tab = pl.BlockSpec((chunk, RET_QK_PAD), lambda b, c: (c, 0))
    st = pl.BlockSpec((None, RET_H, RET_DK, RET_DV), lambda b, c: (b, 0, 0, 0))
    full = lambda shape: pl.BlockSpec(shape, lambda b, c: (0,) * len(shape))
    y, s_last = pl.pallas_call(
        _retention_kernel,
        grid=(batch, nc),
        in_specs=[zq, zk, zv, zg, tab, tab, st, full(intra.shape), full(qdec.shape), full(kdec.shape),
                  full(cdec.shape), full((1, RET_W))],
        out_specs=[pl.BlockSpec((chunk, RET_W), lambda b, c: (row(b, c), 0)), st],
        out_shape=[jax.ShapeDtypeStruct((batch * seq, RET_W), F32),
                   jax.ShapeDtypeStruct((batch, RET_H, RET_DK, RET_DV), F32)],
        scratch_shapes=[pltpu.VMEM((RET_H, RET_DK, RET_DV), F32)],
        compiler_params=_params(("arbitrary", "arbitrary")),
        name="retention",
    )(z, z, z, z, cos, sin, s0, intra, qdec, kdec, cdec, gn.reshape(1, RET_W))
    return y, s_last


def _retention_consts(seq):
    chunk = min(RET_CHUNK, seq)
    log_g = jnp.log1p(-jnp.exp2(-5.0 - jnp.arange(RET_H, dtype=F32)))
    idx = jnp.arange(chunk, dtype=F32)
    diff = idx[:, None] - idx[None, :]
    intra = jnp.where(diff >= 0, jnp.exp(log_g[:, None, None] * jnp.maximum(diff, 0.0)), 0.0)
    q_dec = jnp.exp(log_g[:, None] * (idx + 1.0))
    k_dec = jnp.exp(log_g[:, None] * (chunk - 1.0 - idx))
    chunk_dec = jnp.exp(log_g * chunk)
    expand = lambda t: jnp.pad(jnp.repeat(t.T, RET_DK, axis=1), ((0, 0), (0, RET_QK_PAD - RET_H * RET_DK)))
    cdec = jnp.broadcast_to(chunk_dec[:, None, None], (RET_H, RET_DK, RET_DV))
    return intra, expand(q_dec), expand(k_dec), cdec


def _mla_common(qa_ref, kva_ref, kra_ref, qg_ref, kvg_ref, wuq_ref, cq_ref, sq_ref, ckr_ref, skr_ref):
    cq = _rms(qa_ref[...], qg_ref[...]).astype(BF16)
    qh = _dot(cq, wuq_ref[...])
    cos = cq_ref[...]
    sin = sq_ref[...]
    scale = (MLA_NOPE + MLA_ROPE) ** -0.5
    q_heads = [_rope(qh[:, h * HEAD_PAD:(h + 1) * HEAD_PAD], cos, sin) * scale for h in range(MLA_H)]
    c_new = _rms(kva_ref[...], kvg_ref[...])
    kr_new = _rope(kra_ref[...], ckr_ref[...], skr_ref[...])
    return q_heads, c_new, kr_new


def _mla_prep_prompt_kernel(qa_ref, kva_ref, kra_ref, qg_ref, kvg_ref, wuq_ref, cq_ref, sq_ref, ckr_ref,
                            skr_ref, wk_ref, wvt_ref, c_ref, kr_ref, q_ref, k_ref, vt_ref):
    q_heads, c_new, kr_new = _mla_common(qa_ref, kva_ref, kra_ref, qg_ref, kvg_ref, wuq_ref, cq_ref, sq_ref,
                                         ckr_ref, skr_ref)
    c_ref[...] = c_new
    kr_ref[...] = kr_new[:, :MLA_ROPE]
    for h in range(MLA_H):
        q_ref[:, h * HEAD_PAD:(h + 1) * HEAD_PAD] = (q_heads[h] * LOG2_E).astype(BF16)
    ckr = jnp.concatenate([c_new, kr_new], axis=1).astype(BF16)
    k_ref[...] = _dot(ckr, wk_ref[...]).astype(BF16)
    vt = _dot_nt(wvt_ref[...], ckr[:, :MLA_KV_LORA])
    row = lax.broadcasted_iota(jnp.int32, vt.shape, 0) % VT_ROWS
    vt_ref[...] = jnp.where(row == MLA_DV, 1.0, vt).astype(BF16)


def _mla_prep_sample_kernel(qa_ref, kva_ref, kra_ref, qg_ref, kvg_ref, wuq_ref, cq_ref, sq_ref, ckr_ref,
                            skr_ref, wabs_ref, c_ref, kr_ref, q_ref):
    q_heads, c_new, kr_new = _mla_common(qa_ref, kva_ref, kra_ref, qg_ref, kvg_ref, wuq_ref, cq_ref, sq_ref,
                                         ckr_ref, skr_ref)
    c_ref[...] = c_new
    kr_ref[...] = kr_new[:, :MLA_ROPE]
    for h in range(MLA_H):
        q_ref[:, h * QABS_W:(h + 1) * QABS_W] = _dot(q_heads[h].astype(BF16), wabs_ref[h]).astype(BF16)


def _mla_prep(z, tabs, qg, kvg, wuq_p, extra, prompt):
    rows = z.shape[0]
    tm = min(512, rows)
    cq, sq, ckr, skr = tabs
    zc = lambda c, w: pl.BlockSpec((tm, w), lambda i, c=c, w=w: (i, c // w))
    tab = pl.BlockSpec((tm, LANES), lambda i: (i, 0))
    full = lambda shape: pl.BlockSpec(shape, lambda i: (0,) * len(shape))
    in_specs = [zc(Z_QA, MLA_Q_LORA), zc(Z_KVA, MLA_KV_LORA), zc(Z_KRA, LANES), full((1, MLA_Q_LORA)),
                full((1, MLA_KV_LORA)), full(wuq_p.shape), tab, tab, tab, tab] + [full(e.shape) for e in extra]
    rowblk = lambda w: pl.BlockSpec((tm, w), lambda i: (i, 0))
    out_specs = [rowblk(MLA_KV_LORA), rowblk(MLA_ROPE)]
    out_shape = [jax.ShapeDtypeStruct((rows, MLA_KV_LORA), F32), jax.ShapeDtypeStruct((rows, MLA_ROPE), F32)]
    if prompt:
        widths = [MLA_H * HEAD_PAD] * 2
        body = _mla_prep_prompt_kernel
    else:
        widths = [MLA_H * QABS_W]
        body = _mla_prep_sample_kernel
    out_specs += [rowblk(w) for w in widths]
    out_shape += [jax.ShapeDtypeStruct((rows, w), BF16) for w in widths]
    if prompt:
        out_specs.append(pl.BlockSpec((MLA_H * VT_ROWS, tm), lambda i: (0, i)))
        out_shape.append(jax.ShapeDtypeStruct((MLA_H * VT_ROWS, rows), BF16))
    return pl.pallas_call(
        body,
        grid=(rows // tm,),
        in_specs=in_specs,
        out_specs=out_specs,
        out_shape=out_shape,
        compiler_params=_params(("parallel",)),
        name="mla_prep_prompt" if prompt else "mla_prep_sample",
    )(z, z, z, qg.reshape(1, -1), kvg.reshape(1, -1), wuq_p, cq, sq, ckr, skr, *extra)


def _flash_kernel(qi_ref, ki_ref, q_ref, k_ref, vt_ref, o_ref, m_ref, acc_ref):
    qi = qi_ref[pl.program_id(0)]
    ki = ki_ref[pl.program_id(0)]
    tq = q_ref.shape[0]
    tk = k_ref.shape[0]

    @pl.when(ki == 0)
    def _():
        m_ref[...] = jnp.full_like(m_ref, -jnp.inf)
        acc_ref[...] = jnp.zeros_like(acc_ref)

    def update(masked):
        chains = [(h, c) for h in range(MLA_H) for c in range(tq // FLASH_STRIP)]

        def scores(h, c):
            hs = slice(h * HEAD_PAD, (h + 1) * HEAD_PAD)
            return _dot_nt(k_ref[:, hs], q_ref[c * FLASH_STRIP:(c + 1) * FLASH_STRIP, hs])

        pending = [scores(*ch) for ch in chains[:FLASH_AHEAD]]
        for n, (h, c) in enumerate(chains):
            cs = slice(c * FLASH_STRIP, (c + 1) * FLASH_STRIP)
            s = pending.pop(0)
            if n + FLASH_AHEAD < len(chains):
                pending.append(scores(*chains[n + FLASH_AHEAD]))
            if masked:
                key = lax.broadcasted_iota(jnp.int32, s.shape, 0)
                query = lax.broadcasted_iota(jnp.int32, s.shape, 1) + c * FLASH_STRIP
                s = jnp.where(key <= query, s, -jnp.inf)
            m_old = m_ref[h, :, cs]
            m_new = jnp.maximum(m_old, jnp.max(s, axis=0, keepdims=True))
            p = jnp.exp2(s - m_new)
            alpha = jnp.exp2(m_old - m_new)
            pv = _dot(vt_ref[h * VT_ROWS:(h + 1) * VT_ROWS, :], p.astype(BF16))
            acc_ref[h, :, cs] = alpha * acc_ref[h, :, cs] + pv
            m_ref[h, :, cs] = m_new

    @pl.when(ki < qi)
    def _():
        update(False)

    @pl.when(ki == qi)
    def _():
        update(True)
        for h in range(MLA_H):
            acc = acc_ref[h]
            o_ref[:, h * MLA_DV:(h + 1) * MLA_DV] = (acc[:MLA_DV] / acc[MLA_DV:MLA_DV + 1]).T


def _flash(q, k, vt):
    rows = q.shape[0]
    t = min(512, rows)
    n = rows // t
    w = MLA_H * HEAD_PAD
    pairs = [(i, j) for i in range(n) for j in range(i + 1)]
    qi_tab = jnp.asarray([p[0] for p in pairs], jnp.int32)
    ki_tab = jnp.asarray([p[1] for p in pairs], jnp.int32)
    grid_spec = pltpu.PrefetchScalarGridSpec(
        num_scalar_prefetch=2,
        grid=(len(pairs),),
        in_specs=[pl.BlockSpec((t, w), lambda s, qi, ki: (qi[s], 0)),
                  pl.BlockSpec((t, w), lambda s, qi, ki: (ki[s], 0)),
                  pl.BlockSpec((MLA_H * VT_ROWS, t), lambda s, qi, ki: (0, ki[s]))],
        out_specs=pl.BlockSpec((t, MLA_W), lambda s, qi, ki: (qi[s], 0)),
        scratch_shapes=[pltpu.VMEM((MLA_H, 1, t), F32), pltpu.VMEM((MLA_H, VT_ROWS, t), F32)],
    )
    return pl.pallas_call(
        _flash_kernel,
        grid_spec=grid_spec,
        out_shape=jax.ShapeDtypeStruct((rows, MLA_W), F32),
        compiler_params=_params(("arbitrary",)),
        name="flash_prompt",
    )(qi_tab, ki_tab, q, k, vt)


def _paged_kernel(layer, pages, group, nch, pt_ref, q_ref, cn_ref, krn_ref, lat_hbm, krt_hbm, o_ref,
                  latbuf, krtbuf, lat_sem, krt_sem, m_ref, l_ref, acc_ref):
    s = pl.program_id(0)
    n_steps = pl.num_programs(0)
    ji = s % nch
    slot = s % 2

    def page_copies(step, dst_slot):
        out = []
        for i in range(pages):
            pid = pt_ref[step * pages + i]
            out.append(pltpu.make_async_copy(lat_hbm.at[layer, pid], latbuf.at[dst_slot, i], lat_sem.at[dst_slot]))
            out.append(pltpu.make_async_copy(krt_hbm.at[layer, pid], krtbuf.at[dst_slot, i], krt_sem.at[dst_slot]))
        return out

    def wait_slot(dst_slot):
        for i in range(pages):
            pltpu.make_async_copy(lat_hbm.at[layer, 0], latbuf.at[dst_slot, i], lat_sem.at[dst_slot]).wait()
            pltpu.make_async_copy(krt_hbm.at[layer, 0], krtbuf.at[dst_slot, i], krt_sem.at[dst_slot]).wait()

    @pl.when(s == 0)
    def _():
        for cp in page_copies(0, 0):
            cp.start()

    nxt = jnp.minimum(s + 1, n_steps - 1)
    for cp in page_copies(nxt, 1 - slot):
        cp.start()
    wait_slot(slot)
    lat_refs = [latbuf.at[slot, i] for i in range(pages)]
    krt_refs = [krtbuf.at[slot, i] for i in range(pages)]

    @pl.when(ji == 0)
    def _():
        m_ref[...] = jnp.full_like(m_ref, -jnp.inf)
        l_ref[...] = jnp.zeros_like(l_ref)
        acc_ref[...] = jnp.zeros_like(acc_ref)

    q_lat = q_ref[:, :MLA_KV_LORA]
    q_pe = q_ref[:, MLA_KV_LORA:MLA_KV_LORA + MLA_ROPE]

    def local_softmax(s, values):
        m = jnp.max(s, axis=1, keepdims=True)
        p = jnp.exp(s - m)
        return m, jnp.sum(p, axis=1, keepdims=True), _dot(p.astype(BF16), values)

    def merge(parts):
        m_old = m_ref[...]
        m_new = m_old
        for m, _, _ in parts:
            m_new = jnp.maximum(m_new, m)
        alpha = jnp.exp(m_old - m_new)
        l_new = alpha * l_ref[...]
        acc = alpha * acc_ref[...]
        for m, l, o in parts:
            w = jnp.exp(m - m_new)
            l_new = l_new + w * l
            acc = acc + w * o
        m_ref[...] = m_new
        l_ref[...] = l_new
        acc_ref[...] = acc

    values, scores = [], []
    for g in range(pages // group):
        ids = range(g * group, (g + 1) * group)
        c = jnp.concatenate([lat_refs[i][...].astype(BF16) for i in ids], axis=0)
        krt = jnp.concatenate([krt_refs[i][...].astype(BF16) for i in ids], axis=1)
        values.append(c)
        scores.append(_dot_nt(q_lat, c) + _dot(q_pe, krt))
    merge([local_softmax(s, c) for s, c in zip(scores, values)])

    @pl.when(ji == nch - 1)
    def _():
        c_new = cn_ref[...].astype(BF16)
        s_new = _dot_nt(q_lat, c_new) + _dot_nt(q_pe, krn_ref[...].astype(BF16))
        t_q = lax.broadcasted_iota(jnp.int32, s_new.shape, 0) % SUBLANES
        t_k = lax.broadcasted_iota(jnp.int32, s_new.shape, 1)
        merge([local_softmax(jnp.where(t_k <= t_q, s_new, -jnp.inf), c_new)])
        o_ref[...] = acc_ref[...] / l_ref[...]

    @pl.when(s == n_steps - 1)
    def _():
        wait_slot(1 - slot)


def _paged_attention(layer, q, c_new, kr_new, cache_lat, cache_krt, page_table):
    batch, rows, _ = q.shape
    t_new = c_new.shape[1]
    n_pages = page_table.shape[1]
    page = cache_lat.shape[2]
    pages = min(16, n_pages)
    group = min(4, pages)
    nch = n_pages // pages

    per_b = lambda r, w: pl.BlockSpec((None, r, w), lambda s, pt: (s // nch, 0, 0))
    grid_spec = pltpu.PrefetchScalarGridSpec(
        num_scalar_prefetch=1,
        grid=(batch * nch,),
        in_specs=[per_b(rows, QABS_W), per_b(t_new, MLA_KV_LORA), per_b(t_new, MLA_ROPE),
                  pl.BlockSpec(memory_space=pl.ANY), pl.BlockSpec(memory_space=pl.ANY)],
        out_specs=per_b(rows, MLA_KV_LORA),
        scratch_shapes=[pltpu.VMEM((2, pages, page, MLA_KV_LORA), F32), pltpu.VMEM((2, pages, MLA_ROPE, page), F32),
                        pltpu.SemaphoreType.DMA((2,)), pltpu.SemaphoreType.DMA((2,)),
                        pltpu.VMEM((rows, 1), F32), pltpu.VMEM((rows, 1), F32),
                        pltpu.VMEM((rows, MLA_KV_LORA), F32)],
    )
    return pl.pallas_call(
        functools.partial(_paged_kernel, layer, pages, group, nch),
        grid_spec=grid_spec,
        out_shape=jax.ShapeDtypeStruct((batch, rows, MLA_KV_LORA), F32),
        compiler_params=_params(("arbitrary",)),
        name="paged_sample",
    )(page_table.reshape(-1), q, c_new, kr_new, cache_lat, cache_krt)


def _uv_kernel(o_ref, w_ref, y_ref):
    nb = o_ref.shape[0]
    t = o_ref.shape[2]
    for h in range(MLA_H):
        o_h = o_ref[:, h].reshape(nb * t, MLA_KV_LORA).astype(BF16)
        y_ref[:, h * MLA_DV:(h + 1) * MLA_DV] = _dot(o_h, w_ref[h])


def _uv_proj(o_lat, w_uv_h):
    batch, _, t, _ = o_lat.shape
    return pl.pallas_call(
        _uv_kernel,
        grid=(1,),
        in_specs=[pl.BlockSpec(o_lat.shape, lambda i: (0, 0, 0, 0)),
                  pl.BlockSpec(w_uv_h.shape, lambda i: (0, 0, 0))],
        out_specs=pl.BlockSpec((batch * t, MLA_W), lambda i: (0, 0)),
        out_shape=jax.ShapeDtypeStruct((batch * t, MLA_W), F32),
        compiler_params=_params(("arbitrary",)),
        name="uv_proj",
    )(o_lat, w_uv_h)


def _out_proj_kernel(final, x_ref, ya_ref, yb_ref, oc_ref, gm_ref, w_ref, fg_ref, *out_refs):
    yc = oc_ref[...] * _silu(gm_ref[...])
    upd = _dot(ya_ref[...].astype(BF16), w_ref[0:LRU_W, :])
    upd = upd + _dot(yb_ref[...].astype(BF16), w_ref[LRU_W:LRU_W + RET_W, :])
    upd = upd + _dot(yc.astype(BF16), w_ref[LRU_W + RET_W:, :])
    x_new = x_ref[...] + upd
    out_refs[0][...] = x_new
    if final:
        out_refs[1][...] = _rms(x_new, fg_ref[...])


def _out_proj(x2d, y_a, y_b, o_c, z, w_out, final_g, final):
    rows = x2d.shape[0]
    tm = min(512, rows)
    rowblk = lambda w: pl.BlockSpec((tm, w), lambda i: (i, 0))
    full = lambda shape: pl.BlockSpec(shape, lambda i: (0,) * len(shape))
    n_out = 2 if final else 1
    outs = pl.pallas_call(
        functools.partial(_out_proj_kernel, final),
        grid=(rows // tm,),
        in_specs=[rowblk(D_MODEL), rowblk(LRU_W), rowblk(RET_W), rowblk(MLA_W),
                  pl.BlockSpec((tm, MLA_W), lambda i: (i, Z_GM // MLA_W)), full(w_out.shape), full((1, D_MODEL))],
        out_specs=[rowblk(D_MODEL)] * n_out,
        out_shape=[jax.ShapeDtypeStruct((rows, D_MODEL), F32)] * n_out,
        compiler_params=_params(("parallel",)),
        name="out_proj_final" if final else "out_proj",
    )(x2d, y_a, y_b, o_c, z, w_out, final_g.reshape(1, D_MODEL))
    return outs


def _pad_cols(w, width):
    return jnp.pad(w, ((0, 0), (0, width - w.shape[1])))


def _layout_w_in(w):
    splits = np.cumsum([0, LRU_W, LRU_W, RET_H * RET_DK, RET_H * RET_DK, RET_W, RET_W, MLA_Q_LORA, MLA_KV_LORA,
                        MLA_ROPE, MLA_W])
    seg = [w[:, splits[i]:splits[i + 1]] for i in range(10)]
    x_lru, g_lru, q_r, k_r, v_r, g_r, q_a, kv_a, kr_a, g_m = seg
    cols = [x_lru, g_lru, _pad_cols(q_r, RET_QK_PAD), _pad_cols(k_r, RET_QK_PAD), q_a, kv_a, v_r, g_r, g_m,
            _pad_cols(kr_a, LANES)]
    return jnp.concatenate(cols, axis=1).astype(BF16)


def _block_diag(w):
    n, c, d = w.shape
    eye = jnp.eye(n, dtype=w.dtype)
    return (eye[:, None, :, None] * w[:, :, None, :]).reshape(n * c, n * d)


def _layout_w_uq(w):
    per_head = w.reshape(MLA_Q_LORA, MLA_H, MLA_NOPE + MLA_ROPE)
    per_head = jnp.pad(per_head, ((0, 0), (0, 0), (0, HEAD_PAD - MLA_NOPE - MLA_ROPE)))
    return per_head.reshape(MLA_Q_LORA, MLA_H * HEAD_PAD).astype(BF16)


def _layout_w_key(w_uk):
    top = jnp.pad(w_uk, ((0, 0), (0, 0), (0, HEAD_PAD - MLA_NOPE)))
    rope_rows = jnp.zeros((LANES, MLA_H, HEAD_PAD), F32)
    eye = jnp.eye(MLA_ROPE, dtype=F32)
    rope_rows = rope_rows.at[:MLA_ROPE, :, MLA_NOPE:MLA_NOPE + MLA_ROPE].set(
        jnp.broadcast_to(eye[:, None, :], (MLA_ROPE, MLA_H, MLA_ROPE)))
    return jnp.concatenate([top, rope_rows], axis=0).reshape(MLA_KV_LORA + LANES, MLA_H * HEAD_PAD).astype(BF16)


def _layout_w_val_t(w_uv):
    w = jnp.pad(w_uv.transpose(1, 2, 0), ((0, 0), (0, VT_ROWS - MLA_DV), (0, 0)))
    return w.reshape(MLA_H * VT_ROWS, MLA_KV_LORA).astype(BF16)


def _layout_w_abs(w_uk):
    w = jnp.zeros((MLA_H, HEAD_PAD, QABS_W), F32)
    w = w.at[:, :MLA_NOPE, :MLA_KV_LORA].set(w_uk.transpose(1, 2, 0))
    w = w.at[:, MLA_NOPE:MLA_NOPE + MLA_ROPE, MLA_KV_LORA:MLA_KV_LORA + MLA_ROPE].set(
        jnp.broadcast_to(jnp.eye(MLA_ROPE, dtype=F32), (MLA_H, MLA_ROPE, MLA_ROPE)))
    return w.astype(BF16)


def _rope_tables(pos):
    half = MLA_ROPE // 2
    inv = ROPE_BASE ** (-jnp.arange(half, dtype=F32) / half)
    ang = pos[:, None] * inv[None, :]
    cos = jnp.cos(ang)
    sin = jnp.sin(ang)
    c32 = jnp.concatenate([cos, cos], axis=1)
    s32 = jnp.concatenate([-sin, sin], axis=1)
    t = pos.shape[0]
    ret_c = _pad_cols(jnp.tile(c32, (1, RET_H)), RET_QK_PAD)
    ret_s = _pad_cols(jnp.tile(s32, (1, RET_H)), RET_QK_PAD)
    q_c = jnp.concatenate([jnp.ones((t, MLA_NOPE), F32), c32, jnp.zeros((t, HEAD_PAD - MLA_NOPE - MLA_ROPE), F32)], axis=1)
    q_s = jnp.concatenate([jnp.zeros((t, MLA_NOPE), F32), s32, jnp.zeros((t, HEAD_PAD - MLA_NOPE - MLA_ROPE), F32)], axis=1)
    kr_c = _pad_cols(c32, LANES)
    kr_s = _pad_cols(s32, LANES)
    return (ret_c, ret_s), (q_c, q_s, kr_c, kr_s)


def kernel(x_prompt, x_sample, cache_mla_latent, cache_mla_krope, state_ret, state_lru_h, state_conv, page_table,
           norm_g, w_in, conv_w, conv_b, lru_wa, lru_ba, lru_wx, lru_bx, lru_lambda, ret_gn_g, q_norm_g, w_uq,
           kv_norm_g, w_uk, w_uv, w_out, final_norm_g):
    bp, tp, _ = x_prompt.shape
    bs, ts, _ = x_sample.shape
    depth = w_in.shape[0]
    past_len = page_table.shape[1] * cache_mla_latent.shape[2]
    assert bp == 1 and ts == SUBLANES

    pos_p = jnp.arange(tp, dtype=F32)
    pos_s = past_len + jnp.arange(ts, dtype=F32)
    ret_tab_p, mla_tab_p = _rope_tables(pos_p)
    ret_tab_s, mla_tab_s = _rope_tables(pos_s)
    ret_const_p = _retention_consts(tp)
    ret_const_s = _retention_consts(ts)

    cache_krt = jnp.swapaxes(cache_mla_krope, 2, 3)

    xp = x_prompt.reshape(bp * tp, D_MODEL)
    xs = x_sample.reshape(bs * ts, D_MODEL)
    yp = ys = None
    outs = [[] for _ in range(10)]
    for l in range(depth):
        final = l == depth - 1
        w_in_p = _layout_w_in(w_in[l])
        lw = (conv_w[l], conv_b[l].reshape(1, -1), _block_diag(lru_wa[l]).astype(BF16), lru_ba[l].reshape(1, -1),
              _block_diag(lru_wx[l]).astype(BF16), lru_bx[l].reshape(1, -1), lru_lambda[l].reshape(1, -1))
        wuq_p = _layout_w_uq(w_uq[l])
        w_out_b = w_out[l].astype(BF16)

        z = _in_proj(xp, norm_g[l], w_in_p)
        y_a, h_last = _lru_prompt(z, lw)
        y_b, s_last = _retention(z, *ret_tab_p, jnp.zeros((bp, RET_H, RET_DK, RET_DV), F32), ret_const_p,
                                 ret_gn_g[l], bp, tp)
        c_new, kr_new, q, k, vt = _mla_prep(z, mla_tab_p, q_norm_g[l], kv_norm_g[l], wuq_p,
                                            (_layout_w_key(w_uk[l]), _layout_w_val_t(w_uv[l])), True)
        o_c = _flash(q, k, vt)
        res = _out_proj(xp, y_a, y_b, o_c, z, w_out_b, final_norm_g, final)
        xp = res[0]
        if final:
            yp = res[1]
        conv_rows = z[:, Z_XLRU:Z_XLRU + LRU_W].reshape(bp, tp, LRU_W)[:, tp - (CONV_K - 1):, :]
        for lst, val in zip(outs[:5], (c_new.reshape(bp, tp, -1), kr_new.reshape(bp, tp, -1), s_last,
                                       h_last.reshape(bp, LRU_W), conv_rows)):
            lst.append(val)

        z = _in_proj(xs, norm_g[l], w_in_p)
        buf = jnp.pad(state_conv[l], ((0, 0), (SUBLANES - (CONV_K - 1), 0), (0, 0)))
        xprev = jnp.roll(buf, -1, axis=0).reshape(bs * ts, LRU_W)
        y_a, h_last = _lru_sample(z, xprev, state_lru_h[l], lw)
        y_b, s_last = _retention(z, *ret_tab_s, state_ret[l], ret_const_s, ret_gn_g[l], bs, ts)
        tabs_s = tuple(jnp.tile(t, (bs, 1)) for t in mla_tab_s)
        c_new, kr_new, q_abs = _mla_prep(z, tabs_s, q_norm_g[l], kv_norm_g[l], wuq_p, (_layout_w_abs(w_uk[l]),), False)
        q_abs = q_abs.reshape(bs, ts, MLA_H, QABS_W).transpose(0, 2, 1, 3).reshape(bs, MLA_H * ts, QABS_W)
        o_lat = _paged_attention(l, q_abs, c_new.reshape(bs, ts, -1), kr_new.reshape(bs, ts, -1),
                                 cache_mla_latent, cache_krt, page_table)
        o_c = _uv_proj(o_lat.reshape(bs, MLA_H, ts, MLA_KV_LORA), w_uv[l].transpose(1, 0, 2).astype(BF16))
        res = _out_proj(xs, y_a, y_b, o_c, z, w_out_b, final_norm_g, final)
        xs = res[0]
        if final:
            ys = res[1]
        conv_rows = z[:, Z_XLRU:Z_XLRU + LRU_W].reshape(bs, ts, LRU_W)[:, ts - (CONV_K - 1):, :]
        for lst, val in zip(outs[5:], (c_new.reshape(bs, ts, -1), kr_new.reshape(bs, ts, -1), s_last, h_last,
                                       conv_rows)):
            lst.append(val)

    return (yp.reshape(bp, tp, D_MODEL), ys.reshape(bs, ts, D_MODEL)) + tuple(jnp.stack(o) for o in outs)
```

```python
import functools

import jax
import jax.numpy as jnp
import numpy as np
from jax import lax
from jax.experimental import pallas as pl
from jax.experimental.pallas import tpu as pltpu

F32 = jnp.float32
BF16 = jnp.bfloat16

D_MODEL = 1024
LRU_W = 256
LRU_BLOCKS = 4
CONV_K = 4
LRU_C = 8.0
RET_H = 6
RET_DK = 32
RET_DV = 64
RET_W = RET_H * RET_DV
RET_CHUNK = 128
MLA_H = 6
MLA_NOPE = 64
MLA_ROPE = 32
MLA_DV = 64
MLA_W = MLA_H * MLA_DV
MLA_Q_LORA = 256
MLA_KV_LORA = 256
ROPE_BASE = 10000.0
NORM_EPS = 1e-6
LOG2_E = 1.4426950408889634

LANES = 128
SUBLANES = 8
HEAD_PAD = 128
BF16_ROWS = 16
VT_ROWS = MLA_DV + BF16_ROWS
QABS_W = 384
RET_SEQS_PER_STEP = 8
PAGED_AHEAD = 2
PAGED_SLOTS = PAGED_AHEAD + 1
FLASH_STRIP = 512
FLASH_AHEAD = 2
VMEM_LIMIT = 48 * 1024 * 1024

Z_XLRU, Z_GLRU, Z_QR, Z_KR, Z_QA, Z_KVA = 0, 256, 512, 768, 1024, 1280
Z_VR, Z_GR, Z_GM, Z_KRA = 1536, 1920, 2304, 2688
Z_W = 2816
RET_QK_PAD = 256


def _params(sem):
    return pltpu.CompilerParams(dimension_semantics=sem, vmem_limit_bytes=VMEM_LIMIT)


def _rms(x, g):
    return x * lax.rsqrt(jnp.mean(x * x, axis=-1, keepdims=True) + NORM_EPS) * g


def _silu(x):
    return x * jax.nn.sigmoid(x)


def _rope(x, c, s):
    n = x.shape[-1]
    lane = lax.broadcasted_iota(jnp.int32, x.shape, x.ndim - 1)
    up = pltpu.roll(x, n - MLA_ROPE // 2, axis=x.ndim - 1)
    dn = pltpu.roll(x, MLA_ROPE // 2, axis=x.ndim - 1)
    partner = jnp.where((lane % MLA_ROPE) < MLA_ROPE // 2, up, dn)
    return x * c + partner * s


def _dot(a, b):
    return jnp.dot(a, b, preferred_element_type=F32)


def _dot_nt(a, b):
    return lax.dot_general(a, b, (((1,), (1,)), ((), ())), preferred_element_type=F32)


def _dot_tn(a, b):
    return lax.dot_general(a, b, (((0,), (0,)), ((), ())), preferred_element_type=F32)


def _in_proj_kernel(x_ref, g_ref, w_ref, z_ref):
    h = _rms(x_ref[...], g_ref[...])
    z_ref[...] = _dot(h.astype(BF16), w_ref[...])


def _in_proj(x2d, g, w_p):
    rows = x2d.shape[0]
    tm = min(512, rows)
    return pl.pallas_call(
        _in_proj_kernel,
        grid=(rows // tm,),
        in_specs=[
            pl.BlockSpec((tm, D_MODEL), lambda i: (i, 0)),
            pl.BlockSpec((1, D_MODEL), lambda i: (0, 0)),
            pl.BlockSpec((D_MODEL, Z_W), lambda i: (0, 0)),
        ],
        out_specs=pl.BlockSpec((tm, Z_W), lambda i: (i, 0)),
        out_shape=jax.ShapeDtypeStruct((rows, Z_W), F32),
        compiler_params=_params(("parallel",)),
        name="in_proj",
    )(x2d, g.reshape(1, D_MODEL), w_p)


def _group_scan(a, u):
    row = lax.broadcasted_iota(jnp.int32, a.shape, 0) % SUBLANES
    d = 1
    while d < SUBLANES:
        keep = row >= d
        a_prev = jnp.where(keep, pltpu.roll(a, d, axis=0), 1.0)
        u_prev = jnp.where(keep, pltpu.roll(u, d, axis=0), 0.0)
        u = a * u_prev + u
        a = a * a_prev
        d *= 2
    return a, u


def _lru_gates(shifted, cw_ref, cb_ref, wa_ref, ba_ref, wx_ref, bx_ref, lam_ref):
    xc = cb_ref[...] + shifted[3] * cw_ref[0:1, :]
    xc = xc + shifted[2] * cw_ref[1:2, :]
    xc = xc + shifted[1] * cw_ref[2:3, :]
    xc = xc + shifted[0] * cw_ref[3:4, :]
    xcb = xc.astype(BF16)
    gate_r = jax.nn.sigmoid(_dot(xcb, wa_ref[...]) + ba_ref[...])
    gate_i = jax.nn.sigmoid(_dot(xcb, wx_ref[...]) + bx_ref[...])
    neg_lam = -lam_ref[...]
    softplus = jnp.maximum(neg_lam, 0.0) + jnp.log1p(jnp.exp(-jnp.abs(neg_lam)))
    log_a = -LRU_C * gate_r * softplus
    a = jnp.exp(log_a)
    u = jnp.sqrt(-jnp.tanh(log_a) * (a * a + 1.0)) * (gate_i * xc)
    return a, u


def _lru_prompt_kernel(x_ref, g_ref, cw_ref, cb_ref, wa_ref, ba_ref, wx_ref, bx_ref, lam_ref,
                       y_ref, hl_ref, xbuf_ref, a_ref, u_ref, h_ref):
    tm = x_ref.shape[0]

    @pl.when(pl.program_id(0) == 0)
    def _():
        xbuf_ref[0:SUBLANES, :] = jnp.zeros((SUBLANES, LRU_W), F32)
        h_ref[...] = jnp.zeros_like(h_ref)

    xbuf_ref[SUBLANES:, :] = x_ref[...]
    xb = xbuf_ref[...]
    shifted = [x_ref[...]] + [pltpu.roll(xb, d, axis=0)[SUBLANES:, :] for d in range(1, CONV_K)]
    xbuf_ref[0:SUBLANES, :] = x_ref[tm - SUBLANES:, :]

    a, u = _lru_gates(shifted, cw_ref, cb_ref, wa_ref, ba_ref, wx_ref, bx_ref, lam_ref)
    a, u = _group_scan(a, u)
    a_ref[...] = a
    u_ref[...] = u

    def body(g, h_prev):
        r0 = pl.multiple_of(g * SUBLANES, SUBLANES)
        h = u_ref[pl.ds(r0, SUBLANES), :] + a_ref[pl.ds(r0, SUBLANES), :] * h_prev
        u_ref[pl.ds(r0, SUBLANES), :] = h
        return jnp.broadcast_to(h[SUBLANES - 1:SUBLANES, :], (SUBLANES, LRU_W))

    h_last = lax.fori_loop(0, tm // SUBLANES, body, h_ref[...])
    h_ref[...] = h_last
    hl_ref[...] = h_last[0:1, :]
    y_ref[...] = u_ref[...] * _silu(g_ref[...])


def _lru_sample_kernel(x_ref, xprev_ref, g_ref, h0_ref, cw_ref, cb_ref, wa_ref, ba_ref, wx_ref, bx_ref,
                       lam_ref, y_ref, hs_ref):
    rows = x_ref.shape[0]
    x = x_ref[...]
    xprev = xprev_ref[...]
    row = lax.broadcasted_iota(jnp.int32, x.shape, 0) % SUBLANES
    shifted = [x] + [jnp.where(row >= d, pltpu.roll(x, d, axis=0), pltpu.roll(xprev, d, axis=0))
                     for d in range(1, CONV_K)]
    a, u = _lru_gates(shifted, cw_ref, cb_ref, wa_ref, ba_ref, wx_ref, bx_ref, lam_ref)
    a, u = _group_scan(a, u)
    nb = rows // SUBLANES
    h = u.reshape(nb, SUBLANES, LRU_W) + a.reshape(nb, SUBLANES, LRU_W) * h0_ref[...][:, None, :]
    h = h.reshape(rows, LRU_W)
    hs_ref[...] = h
    y_ref[...] = h * _silu(g_ref[...])


def _lru_weight_specs():
    full = lambda shape: pl.BlockSpec(shape, lambda *_: (0,) * len(shape))
    return [full((CONV_K, LRU_W)), full((1, LRU_W)), full((LRU_W, LRU_W)), full((1, LRU_W)),
            full((LRU_W, LRU_W)), full((1, LRU_W)), full((1, LRU_W))]


def _lru_prompt(z, lw):
    rows = z.shape[0]
    tm = min(512, rows)
    col = lambda c: pl.BlockSpec((tm, LRU_W), lambda i, c=c: (i, c // LRU_W))
    y, h_last = pl.pallas_call(
        _lru_prompt_kernel,
        grid=(rows // tm,),
        in_specs=[col(Z_XLRU), col(Z_GLRU)] + _lru_weight_specs(),
        out_specs=[pl.BlockSpec((tm, LRU_W), lambda i: (i, 0)), pl.BlockSpec((1, LRU_W), lambda i: (0, 0))],
        out_shape=[jax.ShapeDtypeStruct((rows, LRU_W), F32), jax.ShapeDtypeStruct((1, LRU_W), F32)],
        scratch_shapes=[pltpu.VMEM((tm + SUBLANES, LRU_W), F32), pltpu.VMEM((tm, LRU_W), F32),
                        pltpu.VMEM((tm, LRU_W), F32), pltpu.VMEM((SUBLANES, LRU_W), F32)],
        compiler_params=_params(("arbitrary",)),
        name="lru_prompt",
    )(z, z, *lw)
    return y, h_last


def _lru_sample(z, xprev, h0, lw):
    rows = z.shape[0]
    nb = rows // SUBLANES
    col = lambda c: pl.BlockSpec((rows, LRU_W), lambda i, c=c: (0, c // LRU_W))
    y, hs = pl.pallas_call(
        _lru_sample_kernel,
        grid=(1,),
        in_specs=[col(Z_XLRU), pl.BlockSpec((rows, LRU_W), lambda i: (0, 0)), col(Z_GLRU),
                  pl.BlockSpec((nb, LRU_W), lambda i: (0, 0))] + _lru_weight_specs(),
        out_specs=[pl.BlockSpec((rows, LRU_W), lambda i: (0, 0))] * 2,
        out_shape=[jax.ShapeDtypeStruct((rows, LRU_W), F32)] * 2,
        compiler_params=_params(("arbitrary",)),
        name="lru_sample",
    )(z, xprev, z, h0, *lw)
    return y, hs.reshape(nb, SUBLANES, LRU_W)[:, SUBLANES - 1, :]


def _retention_kernel(nb, q_ref, k_ref, v_ref, g_ref, c_ref, s_ref, s0_ref, intra_ref, qdec_ref, kdec_ref,
                      cdec_ref, gn_ref, y_ref, sl_ref, st_ref):
    ci = pl.program_id(1)
    chunk = c_ref.shape[0]

    @pl.when(ci == 0)
    def _():
        st_ref[...] = s0_ref[...]

    cos = c_ref[...]
    sin = s_ref[...]
    heads = range(RET_H)
    ks = [slice(h * RET_DK, (h + 1) * RET_DK) for h in heads]
    vs = [slice(h * RET_DV, (h + 1) * RET_DV) for h in heads]
    seqs = []
    for b in range(nb):
        rows = slice(b * chunk, (b + 1) * chunk)
        q = _rope(q_ref[rows, :], cos, sin)
        k = _rope(k_ref[rows, :], cos, sin) * (RET_DK ** -0.5)
        v = v_ref[rows, :]
        qd = q * qdec_ref[...]
        kd = k * kdec_ref[...]
        vh = [v[:, vs[h]].astype(BF16) for h in heads]
        scores = [_dot_nt(q[:, ks[h]].astype(BF16), k[:, ks[h]].astype(BF16)) for h in heads]
        states = [st_ref[b, h] for h in heads]
        cross = [_dot(qd[:, ks[h]].astype(BF16), states[h].astype(BF16)) for h in heads]
        for h in heads:
            st_ref[b, h] = cdec_ref[h] * states[h] + _dot_tn(kd[:, ks[h]].astype(BF16), vh[h])
        seqs.append((rows, vh, scores, cross))
    for rows, vh, scores, cross in seqs:
        gate = _silu(g_ref[rows, :])
        for h in heads:
            o = _dot((scores[h] * intra_ref[h]).astype(BF16), vh[h]) + cross[h]
            o = o * lax.rsqrt(jnp.mean(o * o, axis=-1, keepdims=True) + NORM_EPS) * gn_ref[:, vs[h]]
            y_ref[rows, vs[h]] = o * gate[:, vs[h]]

    @pl.when(ci == pl.num_programs(1) - 1)
    def _():
        sl_ref[...] = st_ref[...]


def _retention(z, cos, sin, s0, consts, gn, batch, seq):
    intra, qdec, kdec, cdec = consts
    chunk = intra.shape[-1]
    nc = seq // chunk
    nb = RET_SEQS_PER_STEP if (nc == 1 and batch % RET_SEQS_PER_STEP == 0) else 1
    rows = nb * chunk
    row = lambda b, c: b * nc + c
    zq = pl.BlockSpec((rows, RET_QK_PAD), lambda b, c: (row(b, c), Z_QR // RET_QK_PAD))
    zk = pl.BlockSpec((rows, RET_QK_PAD), lambda b, c: (row(b, c), Z_KR // RET_QK_PAD))
    zv = pl.BlockSpec((rows, RET_W), lambda b, c: (row(b, c), Z_VR // RET_W))
    zg = pl.BlockSpec((rows, RET_W), lambda b, c: (row(b, c), Z_GR // RET_W))
    tab = pl.BlockSpec((chunk, RET_QK_PAD), lambda b, c: (c, 0))
    st = pl.BlockSpec((nb, RET_H, RET_DK, RET_DV), lambda b, c: (b, 0, 0, 0))
    full = lambda shape: pl.BlockSpec(shape, lambda b, c: (0,) * len(shape))
    y, s_last = pl.pallas_call(
        functools.partial(_retention_kernel, nb),
        grid=(batch // nb, nc),
        in_specs=[zq, zk, zv, zg, tab, tab, st, full(intra.shape), full(qdec.shape), full(kdec.shape),
                  full(cdec.shape), full((1, RET_W))],
        out_specs=[pl.BlockSpec((rows, RET_W), lambda b, c: (row(b, c), 0)), st],
        out_shape=[jax.ShapeDtypeStruct((batch * seq, RET_W), F32),
                   jax.ShapeDtypeStruct((batch, RET_H, RET_DK, RET_DV), F32)],
        scratch_shapes=[pltpu.VMEM((nb, RET_H, RET_DK, RET_DV), F32)],
        compiler_params=_params(("arbitrary", "arbitrary")),
        name="retention",
    )(z, z, z, z, cos, sin, s0, intra, qdec, kdec, cdec, gn.reshape(1, RET_W))
    return y, s_last


def _retention_consts(seq):
    chunk = min(RET_CHUNK, seq)
    log_g = jnp.log1p(-jnp.exp2(-5.0 - jnp.arange(RET_H, dtype=F32)))
    idx = jnp.arange(chunk, dtype=F32)
    diff = idx[:, None] - idx[None, :]
    intra = jnp.where(diff >= 0, jnp.exp(log_g[:, None, None] * jnp.maximum(diff, 0.0)), 0.0)
    q_dec = jnp.exp(log_g[:, None] * (idx + 1.0))
    k_dec = jnp.exp(log_g[:, None] * (chunk - 1.0 - idx))
    chunk_dec = jnp.exp(log_g * chunk)
    expand = lambda t: jnp.pad(jnp.repeat(t.T, RET_DK, axis=1), ((0, 0), (0, RET_QK_PAD - RET_H * RET_DK)))
    cdec = jnp.broadcast_to(chunk_dec[:, None, None], (RET_H, RET_DK, RET_DV))
    return intra, expand(q_dec), expand(k_dec), cdec


def _mla_common(qa_ref, kva_ref, kra_ref, qg_ref, kvg_ref, wuq_ref, cq_ref, sq_ref, ckr_ref, skr_ref):
    cq = _rms(qa_ref[...], qg_ref[...]).astype(BF16)
    qh = _dot(cq, wuq_ref[...])
    cos = cq_ref[...]
    sin = sq_ref[...]
    scale = (MLA_NOPE + MLA_ROPE) ** -0.5
    q_heads = [_rope(qh[:, h * HEAD_PAD:(h + 1) * HEAD_PAD], cos, sin) * scale for h in range(MLA_H)]
    c_new = _rms(kva_ref[...], kvg_ref[...])
    kr_new = _rope(kra_ref[...], ckr_ref[...], skr_ref[...])
    return q_heads, c_new, kr_new


def _mla_prep_prompt_kernel(qa_ref, kva_ref, kra_ref, qg_ref, kvg_ref, wuq_ref, cq_ref, sq_ref, ckr_ref,
                            skr_ref, wk_ref, wvt_ref, c_ref, kr_ref, q_ref, k_ref, vt_ref):
    q_heads, c_new, kr_new = _mla_common(qa_ref, kva_ref, kra_ref, qg_ref, kvg_ref, wuq_ref, cq_ref, sq_ref,
                                         ckr_ref, skr_ref)
    c_ref[...] = c_new
    kr_ref[...] = kr_new[:, :MLA_ROPE]
    for h in range(MLA_H):
        q_ref[:, h * HEAD_PAD:(h + 1) * HEAD_PAD] = (q_heads[h] * LOG2_E).astype(BF16)
    ckr = jnp.concatenate([c_new, kr_new], axis=1).astype(BF16)
    k_ref[...] = _dot(ckr, wk_ref[...]).astype(BF16)
    vt = _dot_nt(wvt_ref[...], ckr[:, :MLA_KV_LORA])
    row = lax.broadcasted_iota(jnp.int32, vt.shape, 0) % VT_ROWS
    vt_ref[...] = jnp.where(row == MLA_DV, 1.0, vt).astype(BF16)


def _mla_prep_sample_kernel(qa_ref, kva_ref, kra_ref, qg_ref, kvg_ref, wuq_ref, cq_ref, sq_ref, ckr_ref,
                            skr_ref, wabs_ref, c_ref, kr_ref, q_ref):
    q_heads, c_new, kr_new = _mla_common(qa_ref, kva_ref, kra_ref, qg_ref, kvg_ref, wuq_ref, cq_ref, sq_ref,
                                         ckr_ref, skr_ref)
    c_ref[...] = c_new
    kr_ref[...] = kr_new[:, :MLA_ROPE]
    for h in range(MLA_H):
        q_ref[:, h * QABS_W:(h + 1) * QABS_W] = _dot(q_heads[h].astype(BF16), wabs_ref[h]).astype(BF16)


def _mla_prep(z, tabs, qg, kvg, wuq_p, extra, prompt):
    rows = z.shape[0]
    tm = min(512, rows)
    cq, sq, ckr, skr = tabs
    zc = lambda c, w: pl.BlockSpec((tm, w), lambda i, c=c, w=w: (i, c // w))
    tab = pl.BlockSpec((tm, LANES), lambda i: (i, 0))
    full = lambda shape: pl.BlockSpec(shape, lambda i: (0,) * len(shape))
    in_specs = [zc(Z_QA, MLA_Q_LORA), zc(Z_KVA, MLA_KV_LORA), zc(Z_KRA, LANES), full((1, MLA_Q_LORA)),
                full((1, MLA_KV_LORA)), full(wuq_p.shape), tab, tab, tab, tab] + [full(e.shape) for e in extra]
    rowblk = lambda w: pl.BlockSpec((tm, w), lambda i: (i, 0))
    out_specs = [rowblk(MLA_KV_LORA), rowblk(MLA_ROPE)]
    out_shape = [jax.ShapeDtypeStruct((rows, MLA_KV_LORA), F32), jax.ShapeDtypeStruct((rows, MLA_ROPE), F32)]
    if prompt:
        widths = [MLA_H * HEAD_PAD] * 2
        body = _mla_prep_prompt_kernel
    else:
        widths = [MLA_H * QABS_W]
        body = _mla_prep_sample_kernel
    out_specs += [rowblk(w) for w in widths]
    out_shape += [jax.ShapeDtypeStruct((rows, w), BF16) for w in widths]
    if prompt:
        out_specs.append(pl.BlockSpec((MLA_H * VT_ROWS, tm), lambda i: (0, i)))
        out_shape.append(jax.ShapeDtypeStruct((MLA_H * VT_ROWS, rows), BF16))
    return pl.pallas_call(
        body,
        grid=(rows // tm,),
        in_specs=in_specs,
        out_specs=out_specs,
        out_shape=out_shape,
        compiler_params=_params(("parallel",)),
        name="mla_prep_prompt" if prompt else "mla_prep_sample",
    )(z, z, z, qg.reshape(1, -1), kvg.reshape(1, -1), wuq_p, cq, sq, ckr, skr, *extra)


def _flash_kernel(qi_ref, ki_ref, q_ref, k_ref, vt_ref, o_ref, m_ref, acc_ref):
    qi = qi_ref[pl.program_id(0)]
    ki = ki_ref[pl.program_id(0)]
    tq = q_ref.shape[0]
    tk = k_ref.shape[0]

    @pl.when(ki == 0)
    def _():
        m_ref[...] = jnp.full_like(m_ref, -jnp.inf)
        acc_ref[...] = jnp.zeros_like(acc_ref)

    def update(masked):
        chains = [(h, c) for h in range(MLA_H) for c in range(tq // FLASH_STRIP)]

        def scores(h, c):
            hs = slice(h * HEAD_PAD, (h + 1) * HEAD_PAD)
            return _dot_nt(k_ref[:, hs], q_ref[c * FLASH_STRIP:(c + 1) * FLASH_STRIP, hs])

        pending = [scores(*ch) for ch in chains[:FLASH_AHEAD]]
        for n, (h, c) in enumerate(chains):
            cs = slice(c * FLASH_STRIP, (c + 1) * FLASH_STRIP)
            s = pending.pop(0)
            if n + FLASH_AHEAD < len(chains):
                pending.append(scores(*chains[n + FLASH_AHEAD]))
            if masked:
                key = lax.broadcasted_iota(jnp.int32, s.shape, 0)
                query = lax.broadcasted_iota(jnp.int32, s.shape, 1) + c * FLASH_STRIP
                s = jnp.where(key <= query, s, -jnp.inf)
            m_old = m_ref[h, :, cs]
            m_new = jnp.maximum(m_old, jnp.max(s, axis=0, keepdims=True))
            p = jnp.exp2(s - m_new)
            alpha = jnp.exp2(m_old - m_new)
            pv = _dot(vt_ref[h * VT_ROWS:(h + 1) * VT_ROWS, :], p.astype(BF16))
            acc_ref[h, :, cs] = alpha * acc_ref[h, :, cs] + pv
            m_ref[h, :, cs] = m_new

    @pl.when(ki < qi)
    def _():
        update(False)

    @pl.when(ki == qi)
    def _():
        update(True)
        for h in range(MLA_H):
            acc = acc_ref[h]
            o_ref[:, h * MLA_DV:(h + 1) * MLA_DV] = (acc[:MLA_DV] / acc[MLA_DV:MLA_DV + 1]).T


def _flash(q, k, vt):
    rows = q.shape[0]
    t = min(512, rows)
    n = rows // t
    w = MLA_H * HEAD_PAD
    pairs = [(i, j) for i in range(n) for j in range(i + 1)]
    qi_tab = jnp.asarray([p[0] for p in pairs], jnp.int32)
    ki_tab = jnp.asarray([p[1] for p in pairs], jnp.int32)
    grid_spec = pltpu.PrefetchScalarGridSpec(
        num_scalar_prefetch=2,
        grid=(len(pairs),),
        in_specs=[pl.BlockSpec((t, w), lambda s, qi, ki: (qi[s], 0)),
                  pl.BlockSpec((t, w), lambda s, qi, ki: (ki[s], 0)),
                  pl.BlockSpec((MLA_H * VT_ROWS, t), lambda s, qi, ki: (0, ki[s]))],
        out_specs=pl.BlockSpec((t, MLA_W), lambda s, qi, ki: (qi[s], 0)),
        scratch_shapes=[pltpu.VMEM((MLA_H, 1, t), F32), pltpu.VMEM((MLA_H, VT_ROWS, t), F32)],
    )
    return pl.pallas_call(
        _flash_kernel,
        grid_spec=grid_spec,
        out_shape=jax.ShapeDtypeStruct((rows, MLA_W), F32),
        compiler_params=_params(("arbitrary",)),
        name="flash_prompt",
    )(qi_tab, ki_tab, q, k, vt)


def _paged_kernel(layer, pages, group, nch, pt_ref, q_ref, cn_ref, krn_ref, lat_hbm, krt_hbm, o_ref,
                  latbuf, krtbuf, lat_sem, krt_sem, m_ref, l_ref, acc_ref):
    s = pl.program_id(0)
    n_steps = pl.num_programs(0)
    ji = s % nch
    slot = s % PAGED_SLOTS

    def page_copies(step, dst_slot):
        out = []
        for i in range(pages):
            pid = pt_ref[step * pages + i]
            out.append(pltpu.make_async_copy(lat_hbm.at[layer, pid], latbuf.at[dst_slot, i], lat_sem.at[dst_slot]))
            out.append(pltpu.make_async_copy(krt_hbm.at[layer, pid], krtbuf.at[dst_slot, i], krt_sem.at[dst_slot]))
        return out

    def wait_slot(dst_slot):
        for i in range(pages):
            pltpu.make_async_copy(lat_hbm.at[layer, 0], latbuf.at[dst_slot, i], lat_sem.at[dst_slot]).wait()
            pltpu.make_async_copy(krt_hbm.at[layer, 0], krtbuf.at[dst_slot, i], krt_sem.at[dst_slot]).wait()

    @pl.when(s == 0)
    def _():
        for a in range(PAGED_AHEAD):
            for cp in page_copies(jnp.minimum(a, n_steps - 1), a):
                cp.start()

    wait_slot(slot)
    ahead_slot = (s + PAGED_AHEAD) % PAGED_SLOTS
    ahead_copies = page_copies(jnp.minimum(s + PAGED_AHEAD, n_steps - 1), ahead_slot)
    lat_refs = [latbuf.at[slot, i] for i in range(pages)]
    krt_refs = [krtbuf.at[slot, i] for i in range(pages)]

    @pl.when(ji == 0)
    def _():
        m_ref[...] = jnp.full_like(m_ref, -jnp.inf)
        l_ref[...] = jnp.zeros_like(l_ref)
        acc_ref[...] = jnp.zeros_like(acc_ref)

    q_lat = q_ref[:, :MLA_KV_LORA]
    q_pe = q_ref[:, MLA_KV_LORA:MLA_KV_LORA + MLA_ROPE]

    def local_softmax(s, values):
        m = jnp.max(s, axis=1, keepdims=True)
        p = jnp.exp(s - m)
        return m, jnp.sum(p, axis=1, keepdims=True), _dot(p.astype(BF16), values)

    def merge(parts):
        m_old = m_ref[...]
        m_new = m_old
        for m, _, _ in parts:
            m_new = jnp.maximum(m_new, m)
        alpha = jnp.exp(m_old - m_new)
        l_new = alpha * l_ref[...]
        acc = alpha * acc_ref[...]
        for m, l, o in parts:
            w = jnp.exp(m - m_new)
            l_new = l_new + w * l
            acc = acc + w * o
        m_ref[...] = m_new
        l_ref[...] = l_new
        acc_ref[...] = acc

    values, scores = [], []
    n_groups = pages // group
    per_group = len(ahead_copies) // n_groups
    for g in range(n_groups):
        ids = range(g * group, (g + 1) * group)
        c = jnp.concatenate([lat_refs[i][...].astype(BF16) for i in ids], axis=0)
        krt = jnp.concatenate([krt_refs[i][...].astype(BF16) for i in ids], axis=1)
        values.append(c)
        scores.append(_dot_nt(q_lat, c) + _dot(q_pe, krt))
        for cp in ahead_copies[g * per_group:(g + 1) * per_group]:
            cp.start()
    merge([local_softmax(s, c) for s, c in zip(scores, values)])

    @pl.when(ji == nch - 1)
    def _():
        c_new = cn_ref[...].astype(BF16)
        s_new = _dot_nt(q_lat, c_new) + _dot_nt(q_pe, krn_ref[...].astype(BF16))
        t_q = lax.broadcasted_iota(jnp.int32, s_new.shape, 0) % SUBLANES
        t_k = lax.broadcasted_iota(jnp.int32, s_new.shape, 1)
        merge([local_softmax(jnp.where(t_k <= t_q, s_new, -jnp.inf), c_new)])
        o_ref[...] = acc_ref[...] / l_ref[...]

    @pl.when(s == n_steps - 1)
    def _():
        for a in range(1, PAGED_AHEAD + 1):
            wait_slot((s + a) % PAGED_SLOTS)


def _paged_attention(layer, q, c_new, kr_new, cache_lat, cache_krt, page_table):
    batch, rows, _ = q.shape
    t_new = c_new.shape[1]
    n_pages = page_table.shape[1]
    page = cache_lat.shape[2]
    pages = min(16, n_pages)
    group = min(4, pages)
    nch = n_pages // pages

    per_b = lambda r, w: pl.BlockSpec((None, r, w), lambda s, pt: (s // nch, 0, 0))
    grid_spec = pltpu.PrefetchScalarGridSpec(
        num_scalar_prefetch=1,
        grid=(batch * nch,),
        in_specs=[per_b(rows, QABS_W), per_b(t_new, MLA_KV_LORA), per_b(t_new, MLA_ROPE),
                  pl.BlockSpec(memory_space=pl.ANY), pl.BlockSpec(memory_space=pl.ANY)],
        out_specs=per_b(rows, MLA_KV_LORA),
        scratch_shapes=[pltpu.VMEM((PAGED_SLOTS, pages, page, MLA_KV_LORA), F32),
                        pltpu.VMEM((PAGED_SLOTS, pages, MLA_ROPE, page), F32),
                        pltpu.SemaphoreType.DMA((PAGED_SLOTS,)), pltpu.SemaphoreType.DMA((PAGED_SLOTS,)),
                        pltpu.VMEM((rows, 1), F32), pltpu.VMEM((rows, 1), F32),
                        pltpu.VMEM((rows, MLA_KV_LORA), F32)],
    )
    return pl.pallas_call(
        functools.partial(_paged_kernel, layer, pages, group, nch),
        grid_spec=grid_spec,
        out_shape=jax.ShapeDtypeStruct((batch, rows, MLA_KV_LORA), F32),
        compiler_params=_params(("arbitrary",)),
        name="paged_sample",
    )(page_table.reshape(-1), q, c_new, kr_new, cache_lat, cache_krt)


def _uv_kernel(o_ref, w_ref, y_ref):
    nb = o_ref.shape[0]
    t = o_ref.shape[2]
    for h in range(MLA_H):
        o_h = o_ref[:, h].reshape(nb * t, MLA_KV_LORA).astype(BF16)
        y_ref[:, h * MLA_DV:(h + 1) * MLA_DV] = _dot(o_h, w_ref[h])


def _uv_proj(o_lat, w_uv_h):
    batch, _, t, _ = o_lat.shape
    return pl.pallas_call(
        _uv_kernel,
        grid=(1,),
        in_specs=[pl.BlockSpec(o_lat.shape, lambda i: (0, 0, 0, 0)),
                  pl.BlockSpec(w_uv_h.shape, lambda i: (0, 0, 0))],
        out_specs=pl.BlockSpec((batch * t, MLA_W), lambda i: (0, 0)),
        out_shape=jax.ShapeDtypeStruct((batch * t, MLA_W), F32),
        compiler_params=_params(("arbitrary",)),
        name="uv_proj",
    )(o_lat, w_uv_h)


def _out_proj_kernel(final, x_ref, ya_ref, yb_ref, oc_ref, gm_ref, w_ref, fg_ref, *out_refs):
    yc = oc_ref[...] * _silu(gm_ref[...])
    upd = _dot(ya_ref[...].astype(BF16), w_ref[0:LRU_W, :])
    upd = upd + _dot(yb_ref[...].astype(BF16), w_ref[LRU_W:LRU_W + RET_W, :])
    upd = upd + _dot(yc.astype(BF16), w_ref[LRU_W + RET_W:, :])
    x_new = x_ref[...] + upd
    out_refs[0][...] = x_new
    if final:
        out_refs[1][...] = _rms(x_new, fg_ref[...])


def _out_proj(x2d, y_a, y_b, o_c, z, w_out, final_g, final):
    rows = x2d.shape[0]
    tm = min(512, rows)
    rowblk = lambda w: pl.BlockSpec((tm, w), lambda i: (i, 0))
    full = lambda shape: pl.BlockSpec(shape, lambda i: (0,) * len(shape))
    n_out = 2 if final else 1
    outs = pl.pallas_call(
        functools.partial(_out_proj_kernel, final),
        grid=(rows // tm,),
        in_specs=[rowblk(D_MODEL), rowblk(LRU_W), rowblk(RET_W), rowblk(MLA_W),
                  pl.BlockSpec((tm, MLA_W), lambda i: (i, Z_GM // MLA_W)), full(w_out.shape), full((1, D_MODEL))],
        out_specs=[rowblk(D_MODEL)] * n_out,
        out_shape=[jax.ShapeDtypeStruct((rows, D_MODEL), F32)] * n_out,
        compiler_params=_params(("parallel",)),
        name="out_proj_final" if final else "out_proj",
    )(x2d, y_a, y_b, o_c, z, w_out, final_g.reshape(1, D_MODEL))
    return outs


def _pad_cols(w, width):
    return jnp.pad(w, ((0, 0), (0, width - w.shape[1])))


def _layout_w_in(w):
    splits = np.cumsum([0, LRU_W, LRU_W, RET_H * RET_DK, RET_H * RET_DK, RET_W, RET_W, MLA_Q_LORA, MLA_KV_LORA,
                        MLA_ROPE, MLA_W])
    seg = [w[:, splits[i]:splits[i + 1]] for i in range(10)]
    x_lru, g_lru, q_r, k_r, v_r, g_r, q_a, kv_a, kr_a, g_m = seg
    cols = [x_lru, g_lru, _pad_cols(q_r, RET_QK_PAD), _pad_cols(k_r, RET_QK_PAD), q_a, kv_a, v_r, g_r, g_m,
            _pad_cols(kr_a, LANES)]
    return jnp.concatenate(cols, axis=1).astype(BF16)


def _block_diag(w):
    n, c, d = w.shape
    eye = jnp.eye(n, dtype=w.dtype)
    return (eye[:, None, :, None] * w[:, :, None, :]).reshape(n * c, n * d)


def _layout_w_uq(w):
    per_head = w.reshape(MLA_Q_LORA, MLA_H, MLA_NOPE + MLA_ROPE)
    per_head = jnp.pad(per_head, ((0, 0), (0, 0), (0, HEAD_PAD - MLA_NOPE - MLA_ROPE)))
    return per_head.reshape(MLA_Q_LORA, MLA_H * HEAD_PAD).astype(BF16)


def _layout_w_key(w_uk):
    top = jnp.pad(w_uk, ((0, 0), (0, 0), (0, HEAD_PAD - MLA_NOPE)))
    rope_rows = jnp.zeros((LANES, MLA_H, HEAD_PAD), F32)
    eye = jnp.eye(MLA_ROPE, dtype=F32)
    rope_rows = rope_rows.at[:MLA_ROPE, :, MLA_NOPE:MLA_NOPE + MLA_ROPE].set(
        jnp.broadcast_to(eye[:, None, :], (MLA_ROPE, MLA_H, MLA_ROPE)))
    return jnp.concatenate([top, rope_rows], axis=0).reshape(MLA_KV_LORA + LANES, MLA_H * HEAD_PAD).astype(BF16)


def _layout_w_val_t(w_uv):
    w = jnp.pad(w_uv.transpose(1, 2, 0), ((0, 0), (0, VT_ROWS - MLA_DV), (0, 0)))
    return w.reshape(MLA_H * VT_ROWS, MLA_KV_LORA).astype(BF16)


def _layout_w_abs(w_uk):
    w = jnp.zeros((MLA_H, HEAD_PAD, QABS_W), F32)
    w = w.at[:, :MLA_NOPE, :MLA_KV_LORA].set(w_uk.transpose(1, 2, 0))
    w = w.at[:, MLA_NOPE:MLA_NOPE + MLA_ROPE, MLA_KV_LORA:MLA_KV_LORA + MLA_ROPE].set(
        jnp.broadcast_to(jnp.eye(MLA_ROPE, dtype=F32), (MLA_H, MLA_ROPE, MLA_ROPE)))
    return w.astype(BF16)


def _rope_tables(pos):
    half = MLA_ROPE // 2
    inv = ROPE_BASE ** (-jnp.arange(half, dtype=F32) / half)
    ang = pos[:, None] * inv[None, :]
    cos = jnp.cos(ang)
    sin = jnp.sin(ang)
    c32 = jnp.concatenate([cos, cos], axis=1)
    s32 = jnp.concatenate([-sin, sin], axis=1)
    t = pos.shape[0]
    ret_c = _pad_cols(jnp.tile(c32, (1, RET_H)), RET_QK_PAD)
    ret_s = _pad_cols(jnp.tile(s32, (1, RET_H)), RET_QK_PAD)
    q_c = jnp.concatenate([jnp.ones((t, MLA_NOPE), F32), c32, jnp.zeros((t, HEAD_PAD - MLA_NOPE - MLA_ROPE), F32)], axis=1)
    q_s = jnp.concatenate([jnp.zeros((t, MLA_NOPE), F32), s32, jnp.zeros((t, HEAD_PAD - MLA_NOPE - MLA_ROPE), F32)], axis=1)
    kr_c = _pad_cols(c32, LANES)
    kr_s = _pad_cols(s32, LANES)
    return (ret_c, ret_s), (q_c, q_s, kr_c, kr_s)


def kernel(x_prompt, x_sample, cache_mla_latent, cache_mla_krope, state_ret, state_lru_h, state_conv, page_table,
           norm_g, w_in, conv_w, conv_b, lru_wa, lru_ba, lru_wx, lru_bx, lru_lambda, ret_gn_g, q_norm_g, w_uq,
           kv_norm_g, w_uk, w_uv, w_out, final_norm_g):
    bp, tp, _ = x_prompt.shape
    bs, ts, _ = x_sample.shape
    depth = w_in.shape[0]
    past_len = page_table.shape[1] * cache_mla_latent.shape[2]
    assert bp == 1 and ts == SUBLANES

    pos_p = jnp.arange(tp, dtype=F32)
    pos_s = past_len + jnp.arange(ts, dtype=F32)
    ret_tab_p, mla_tab_p = _rope_tables(pos_p)
    ret_tab_s, mla_tab_s = _rope_tables(pos_s)
    ret_const_p = _retention_consts(tp)
    ret_const_s = _retention_consts(ts)

    cache_krt = jnp.swapaxes(cache_mla_krope, 2, 3)

    xp = x_prompt.reshape(bp * tp, D_MODEL)
    xs = x_sample.reshape(bs * ts, D_MODEL)
    yp = ys = None
    outs = [[] for _ in range(10)]
    for l in range(depth):
        final = l == depth - 1
        w_in_p = _layout_w_in(w_in[l])
        lw = (conv_w[l], conv_b[l].reshape(1, -1), _block_diag(lru_wa[l]).astype(BF16), lru_ba[l].reshape(1, -1),
              _block_diag(lru_wx[l]).astype(BF16), lru_bx[l].reshape(1, -1), lru_lambda[l].reshape(1, -1))
        wuq_p = _layout_w_uq(w_uq[l])
        w_out_b = w_out[l].astype(BF16)

        z = _in_proj(xp, norm_g[l], w_in_p)
        y_a, h_last = _lru_prompt(z, lw)
        y_b, s_last = _retention(z, *ret_tab_p, jnp.zeros((bp, RET_H, RET_DK, RET_DV), F32), ret_const_p,
                                 ret_gn_g[l], bp, tp)
        c_new, kr_new, q, k, vt = _mla_prep(z, mla_tab_p, q_norm_g[l], kv_norm_g[l], wuq_p,
                                            (_layout_w_key(w_uk[l]), _layout_w_val_t(w_uv[l])), True)
        o_c = _flash(q, k, vt)
        res = _out_proj(xp, y_a, y_b, o_c, z, w_out_b, final_norm_g, final)
        xp = res[0]
        if final:
            yp = res[1]
        conv_rows = z[:, Z_XLRU:Z_XLRU + LRU_W].reshape(bp, tp, LRU_W)[:, tp - (CONV_K - 1):, :]
        for lst, val in zip(outs[:5], (c_new.reshape(bp, tp, -1), kr_new.reshape(bp, tp, -1), s_last,
                                       h_last.reshape(bp, LRU_W), conv_rows)):
            lst.append(val)

        z = _in_proj(xs, norm_g[l], w_in_p)
        buf = jnp.pad(state_conv[l], ((0, 0), (SUBLANES - (CONV_K - 1), 0), (0, 0)))
        xprev = jnp.roll(buf, -1, axis=0).reshape(bs * ts, LRU_W)
        y_a, h_last = _lru_sample(z, xprev, state_lru_h[l], lw)
        y_b, s_last = _retention(z, *ret_tab_s, state_ret[l], ret_const_s, ret_gn_g[l], bs, ts)
        tabs_s = tuple(jnp.tile(t, (bs, 1)) for t in mla_tab_s)
        c_new, kr_new, q_abs = _mla_prep(z, tabs_s, q_norm_g[l], kv_norm_g[l], wuq_p, (_layout_w_abs(w_uk[l]),), False)
        q_abs = q_abs.reshape(bs, ts, MLA_H, QABS_W).transpose(0, 2, 1, 3).reshape(bs, MLA_H * ts, QABS_W)
        o_lat = _paged_attention(l, q_abs, c_new.reshape(bs, ts, -1), kr_new.reshape(bs, ts, -1),
                                 cache_mla_latent, cache_krt, page_table)
        o_c = _uv_proj(o_lat.reshape(bs, MLA_H, ts, MLA_KV_LORA), w_uv[l].transpose(1, 0, 2).astype(BF16))
        res = _out_proj(xs, y_a, y_b, o_c, z, w_out_b, final_norm_g, final)
        xs = res[0]
        if final:
            ys = res[1]
        conv_rows = z[:, Z_XLRU:Z_XLRU + LRU_W].reshape(bs, ts, LRU_W)[:, ts - (CONV_K - 1):, :]
        for lst, val in zip(outs[5:], (c_new.reshape(bs, ts, -1), kr_new.reshape(bs, ts, -1), s_last, h_last,
                                       conv_rows)):
            lst.append(val)

    return (yp.reshape(bp, tp, D_MODEL), ys.reshape(bs, ts, D_MODEL)) + tuple(jnp.stack(o) for o in outs)
```

```python
import functools

import jax
import jax.numpy as jnp
import numpy as np
from jax import lax
from jax.experimental import pallas as pl
from jax.experimental.pallas import tpu as pltpu

F32 = jnp.float32
BF16 = jnp.bfloat16

D_MODEL = 1024
LRU_W = 256
LRU_BLOCKS = 4
CONV_K = 4
LRU_C = 8.0
RET_H = 6
RET_DK = 32
RET_DV = 64
RET_W = RET_H * RET_DV
RET_CHUNK = 128
MLA_H = 6
MLA_NOPE = 64
MLA_ROPE = 32
MLA_DV = 64
MLA_W = MLA_H * MLA_DV
MLA_Q_LORA = 256
MLA_KV_LORA = 256
ROPE_BASE = 10000.0
NORM_EPS = 1e-6
LOG2_E = 1.4426950408889634

LANES = 128
SUBLANES = 8
HEAD_PAD = 128
BF16_ROWS = 16
VT_ROWS = MLA_DV + BF16_ROWS
QABS_W = 384
RET_SEQS_PER_STEP = 8
PAGED_AHEAD = 2
PAGED_SLOTS = PAGED_AHEAD + 1
FLASH_STRIP = 512
FLASH_TQ = 1024
FLASH_AHEAD = 2
VMEM_LIMIT = 48 * 1024 * 1024

Z_XLRU, Z_GLRU, Z_QR, Z_KR, Z_QA, Z_KVA = 0, 256, 512, 768, 1024, 1280
Z_VR, Z_GR, Z_GM, Z_KRA = 1536, 1920, 2304, 2688
Z_W = 2816
RET_QK_PAD = 256


def _params(sem):
    return pltpu.CompilerParams(dimension_semantics=sem, vmem_limit_bytes=VMEM_LIMIT)


def _rms(x, g):
    return x * lax.rsqrt(jnp.mean(x * x, axis=-1, keepdims=True) + NORM_EPS) * g


def _silu(x):
    return x * jax.nn.sigmoid(x)


def _rope(x, c, s):
    n = x.shape[-1]
    lane = lax.broadcasted_iota(jnp.int32, x.shape, x.ndim - 1)
    up = pltpu.roll(x, n - MLA_ROPE // 2, axis=x.ndim - 1)
    dn = pltpu.roll(x, MLA_ROPE // 2, axis=x.ndim - 1)
    partner = jnp.where((lane % MLA_ROPE) < MLA_ROPE // 2, up, dn)
    return x * c + partner * s


def _dot(a, b):
    return jnp.dot(a, b, preferred_element_type=F32)


def _dot_nt(a, b):
    return lax.dot_general(a, b, (((1,), (1,)), ((), ())), preferred_element_type=F32)


def _dot_tn(a, b):
    return lax.dot_general(a, b, (((0,), (0,)), ((), ())), preferred_element_type=F32)


def _in_proj_kernel(x_ref, g_ref, w_ref, z_ref):
    h = _rms(x_ref[...], g_ref[...])
    z_ref[...] = _dot(h.astype(BF16), w_ref[...])


def _in_proj(x2d, g, w_p):
    rows = x2d.shape[0]
    tm = min(512, rows)
    return pl.pallas_call(
        _in_proj_kernel,
        grid=(rows // tm,),
        in_specs=[
            pl.BlockSpec((tm, D_MODEL), lambda i: (i, 0)),
            pl.BlockSpec((1, D_MODEL), lambda i: (0, 0)),
            pl.BlockSpec((D_MODEL, Z_W), lambda i: (0, 0)),
        ],
        out_specs=pl.BlockSpec((tm, Z_W), lambda i: (i, 0)),
        out_shape=jax.ShapeDtypeStruct((rows, Z_W), F32),
        compiler_params=_params(("parallel",)),
        name="in_proj",
    )(x2d, g.reshape(1, D_MODEL), w_p)


def _group_scan(a, u):
    row = lax.broadcasted_iota(jnp.int32, a.shape, 0) % SUBLANES
    d = 1
    while d < SUBLANES:
        keep = row >= d
        a_prev = jnp.where(keep, pltpu.roll(a, d, axis=0), 1.0)
        u_prev = jnp.where(keep, pltpu.roll(u, d, axis=0), 0.0)
        u = a * u_prev + u
        a = a * a_prev
        d *= 2
    return a, u


def _lru_gates(shifted, cw_ref, cb_ref, wa_ref, ba_ref, wx_ref, bx_ref, lam_ref):
    xc = cb_ref[...] + shifted[3] * cw_ref[0:1, :]
    xc = xc + shifted[2] * cw_ref[1:2, :]
    xc = xc + shifted[1] * cw_ref[2:3, :]
    xc = xc + shifted[0] * cw_ref[3:4, :]
    xcb = xc.astype(BF16)
    gate_r = jax.nn.sigmoid(_dot(xcb, wa_ref[...]) + ba_ref[...])
    gate_i = jax.nn.sigmoid(_dot(xcb, wx_ref[...]) + bx_ref[...])
    neg_lam = -lam_ref[...]
    softplus = jnp.maximum(neg_lam, 0.0) + jnp.log1p(jnp.exp(-jnp.abs(neg_lam)))
    log_a = -LRU_C * gate_r * softplus
    a = jnp.exp(log_a)
    u = jnp.sqrt(-jnp.tanh(log_a) * (a * a + 1.0)) * (gate_i * xc)
    return a, u


def _lru_prompt_kernel(x_ref, g_ref, cw_ref, cb_ref, wa_ref, ba_ref, wx_ref, bx_ref, lam_ref,
                       y_ref, hl_ref, xbuf_ref, a_ref, u_ref, h_ref):
    tm = x_ref.shape[0]

    @pl.when(pl.program_id(0) == 0)
    def _():
        xbuf_ref[0:SUBLANES, :] = jnp.zeros((SUBLANES, LRU_W), F32)
        h_ref[...] = jnp.zeros_like(h_ref)

    xbuf_ref[SUBLANES:, :] = x_ref[...]
    xb = xbuf_ref[...]
    shifted = [x_ref[...]] + [pltpu.roll(xb, d, axis=0)[SUBLANES:, :] for d in range(1, CONV_K)]
    xbuf_ref[0:SUBLANES, :] = x_ref[tm - SUBLANES:, :]

    a, u = _lru_gates(shifted, cw_ref, cb_ref, wa_ref, ba_ref, wx_ref, bx_ref, lam_ref)
    a, u = _group_scan(a, u)
    a_ref[...] = a
    u_ref[...] = u

    def body(g, h_prev):
        r0 = pl.multiple_of(g * SUBLANES, SUBLANES)
        h = u_ref[pl.ds(r0, SUBLANES), :] + a_ref[pl.ds(r0, SUBLANES), :] * h_prev
        u_ref[pl.ds(r0, SUBLANES), :] = h
        return jnp.broadcast_to(h[SUBLANES - 1:SUBLANES, :], (SUBLANES, LRU_W))

    h_last = lax.fori_loop(0, tm // SUBLANES, body, h_ref[...])
    h_ref[...] = h_last
    hl_ref[...] = h_last[0:1, :]
    y_ref[...] = u_ref[...] * _silu(g_ref[...])


def _lru_sample_kernel(x_ref, xprev_ref, g_ref, h0_ref, cw_ref, cb_ref, wa_ref, ba_ref, wx_ref, bx_ref,
                       lam_ref, y_ref, hs_ref):
    rows = x_ref.shape[0]
    x = x_ref[...]
    xprev = xprev_ref[...]
    row = lax.broadcasted_iota(jnp.int32, x.shape, 0) % SUBLANES
    shifted = [x] + [jnp.where(row >= d, pltpu.roll(x, d, axis=0), pltpu.roll(xprev, d, axis=0))
                     for d in range(1, CONV_K)]
    a, u = _lru_gates(shifted, cw_ref, cb_ref, wa_ref, ba_ref, wx_ref, bx_ref, lam_ref)
    a, u = _group_scan(a, u)
    nb = rows // SUBLANES
    h = u.reshape(nb, SUBLANES, LRU_W) + a.reshape(nb, SUBLANES, LRU_W) * h0_ref[...][:, None, :]
    h = h.reshape(rows, LRU_W)
    hs_ref[...] = h
    y_ref[...] = h * _silu(g_ref[...])


def _lru_weight_specs():
    full = lambda shape: pl.BlockSpec(shape, lambda *_: (0,) * len(shape))
    return [full((CONV_K, LRU_W)), full((1, LRU_W)), full((LRU_W, LRU_W)), full((1, LRU_W)),
            full((LRU_W, LRU_W)), full((1, LRU_W)), full((1, LRU_W))]


def _lru_prompt(z, lw):
    rows = z.shape[0]
    tm = min(512, rows)
    col = lambda c: pl.BlockSpec((tm, LRU_W), lambda i, c=c: (i, c // LRU_W))
    y, h_last = pl.pallas_call(
        _lru_prompt_kernel,
        grid=(rows // tm,),
        in_specs=[col(Z_XLRU), col(Z_GLRU)] + _lru_weight_specs(),
        out_specs=[pl.BlockSpec((tm, LRU_W), lambda i: (i, 0)), pl.BlockSpec((1, LRU_W), lambda i: (0, 0))],
        out_shape=[jax.ShapeDtypeStruct((rows, LRU_W), F32), jax.ShapeDtypeStruct((1, LRU_W), F32)],
        scratch_shapes=[pltpu.VMEM((tm + SUBLANES, LRU_W), F32), pltpu.VMEM((tm, LRU_W), F32),
                        pltpu.VMEM((tm, LRU_W), F32), pltpu.VMEM((SUBLANES, LRU_W), F32)],
        compiler_params=_params(("arbitrary",)),
        name="lru_prompt",
    )(z, z, *lw)
    return y, h_last


def _lru_sample(z, xprev, h0, lw):
    rows = z.shape[0]
    nb = rows // SUBLANES
    col = lambda c: pl.BlockSpec((rows, LRU_W), lambda i, c=c: (0, c // LRU_W))
    y, hs = pl.pallas_call(
        _lru_sample_kernel,
        grid=(1,),
        in_specs=[col(Z_XLRU), pl.BlockSpec((rows, LRU_W), lambda i: (0, 0)), col(Z_GLRU),
                  pl.BlockSpec((nb, LRU_W), lambda i: (0, 0))] + _lru_weight_specs(),
        out_specs=[pl.BlockSpec((rows, LRU_W), lambda i: (0, 0))] * 2,
        out_shape=[jax.ShapeDtypeStruct((rows, LRU_W), F32)] * 2,
        compiler_params=_params(("arbitrary",)),
        name="lru_sample",
    )(z, xprev, z, h0, *lw)
    return y, hs.reshape(nb, SUBLANES, LRU_W)[:, SUBLANES - 1, :]


def _retention_kernel(nb, q_ref, k_ref, v_ref, g_ref, c_ref, s_ref, s0_ref, intra_ref, qdec_ref, kdec_ref,
                      cdec_ref, gn_ref, y_ref, sl_ref, st_ref):
    ci = pl.program_id(1)
    chunk = c_ref.shape[0]

    @pl.when(ci == 0)
    def _():
        st_ref[...] = s0_ref[...]

    cos = c_ref[...]
    sin = s_ref[...]
    heads = range(RET_H)
    ks = [slice(h * RET_DK, (h + 1) * RET_DK) for h in heads]
    vs = [slice(h * RET_DV, (h + 1) * RET_DV) for h in heads]
    seqs = []
    for b in range(nb):
        rows = slice(b * chunk, (b + 1) * chunk)
        q = _rope(q_ref[rows, :], cos, sin)
        k = _rope(k_ref[rows, :], cos, sin) * (RET_DK ** -0.5)
        v = v_ref[rows, :]
        qd = q * qdec_ref[...]
        kd = k * kdec_ref[...]
        vh = [v[:, vs[h]].astype(BF16) for h in heads]
        scores = [_dot_nt(q[:, ks[h]].astype(BF16), k[:, ks[h]].astype(BF16)) for h in heads]
        states = [st_ref[b, h] for h in heads]
        cross = [_dot(qd[:, ks[h]].astype(BF16), states[h].astype(BF16)) for h in heads]
        for h in heads:
            st_ref[b, h] = cdec_ref[h] * states[h] + _dot_tn(kd[:, ks[h]].astype(BF16), vh[h])
        seqs.append((rows, vh, scores, cross))
    for rows, vh, scores, cross in seqs:
        gate = _silu(g_ref[rows, :])
        for h in heads:
            o = _dot((scores[h] * intra_ref[h]).astype(BF16), vh[h]) + cross[h]
            o = o * lax.rsqrt(jnp.mean(o * o, axis=-1, keepdims=True) + NORM_EPS) * gn_ref[:, vs[h]]
            y_ref[rows, vs[h]] = o * gate[:, vs[h]]

    @pl.when(ci == pl.num_programs(1) - 1)
    def _():
        sl_ref[...] = st_ref[...]


def _retention(z, cos, sin, s0, consts, gn, batch, seq):
    intra, qdec, kdec, cdec = consts
    chunk = intra.shape[-1]
    nc = seq // chunk
    nb = RET_SEQS_PER_STEP if (nc == 1 and batch % RET_SEQS_PER_STEP == 0) else 1
    rows = nb * chunk
    row = lambda b, c: b * nc + c
    zq = pl.BlockSpec((rows, RET_QK_PAD), lambda b, c: (row(b, c), Z_QR // RET_QK_PAD))
    zk = pl.BlockSpec((rows, RET_QK_PAD), lambda b, c: (row(b, c), Z_KR // RET_QK_PAD))
    zv = pl.BlockSpec((rows, RET_W), lambda b, c: (row(b, c), Z_VR // RET_W))
    zg = pl.BlockSpec((rows, RET_W), lambda b, c: (row(b, c), Z_GR // RET_W))
    tab = pl.BlockSpec((chunk, RET_QK_PAD), lambda b, c: (c, 0))
    st = pl.BlockSpec((nb, RET_H, RET_DK, RET_DV), lambda b, c: (b, 0, 0, 0))
    full = lambda shape: pl.BlockSpec(shape, lambda b, c: (0,) * len(shape))
    y, s_last = pl.pallas_call(
        functools.partial(_retention_kernel, nb),
        grid=(batch // nb, nc),
        in_specs=[zq, zk, zv, zg, tab, tab, st, full(intra.shape), full(qdec.shape), full(kdec.shape),
                  full(cdec.shape), full((1, RET_W))],
        out_specs=[pl.BlockSpec((rows, RET_W), lambda b, c: (row(b, c), 0)), st],
        out_shape=[jax.ShapeDtypeStruct((batch * seq, RET_W), F32),
                   jax.ShapeDtypeStruct((batch, RET_H, RET_DK, RET_DV), F32)],
        scratch_shapes=[pltpu.VMEM((nb, RET_H, RET_DK, RET_DV), F32)],
        compiler_params=_params(("arbitrary", "arbitrary")),
        name="retention",
    )(z, z, z, z, cos, sin, s0, intra, qdec, kdec, cdec, gn.reshape(1, RET_W))
    return y, s_last


def _retention_consts(seq):
    chunk = min(RET_CHUNK, seq)
    log_g = jnp.log1p(-jnp.exp2(-5.0 - jnp.arange(RET_H, dtype=F32)))
    idx = jnp.arange(chunk, dtype=F32)
    diff = idx[:, None] - idx[None, :]
    intra = jnp.where(diff >= 0, jnp.exp(log_g[:, None, None] * jnp.maximum(diff, 0.0)), 0.0)
    q_dec = jnp.exp(log_g[:, None] * (idx + 1.0))
    k_dec = jnp.exp(log_g[:, None] * (chunk - 1.0 - idx))
    chunk_dec = jnp.exp(log_g * chunk)
    expand = lambda t: jnp.pad(jnp.repeat(t.T, RET_DK, axis=1), ((0, 0), (0, RET_QK_PAD - RET_H * RET_DK)))
    cdec = jnp.broadcast_to(chunk_dec[:, None, None], (RET_H, RET_DK, RET_DV))
    return intra, expand(q_dec), expand(k_dec), cdec


def _mla_common(qa_ref, kva_ref, kra_ref, qg_ref, kvg_ref, wuq_ref, cq_ref, sq_ref, ckr_ref, skr_ref):
    cq = _rms(qa_ref[...], qg_ref[...]).astype(BF16)
    qh = _dot(cq, wuq_ref[...])
    cos = cq_ref[...]
    sin = sq_ref[...]
    scale = (MLA_NOPE + MLA_ROPE) ** -0.5
    q_heads = [_rope(qh[:, h * HEAD_PAD:(h + 1) * HEAD_PAD], cos, sin) * scale for h in range(MLA_H)]
    c_new = _rms(kva_ref[...], kvg_ref[...])
    kr_new = _rope(kra_ref[...], ckr_ref[...], skr_ref[...])
    return q_heads, c_new, kr_new


def _mla_prep_prompt_kernel(qa_ref, kva_ref, kra_ref, qg_ref, kvg_ref, wuq_ref, cq_ref, sq_ref, ckr_ref,
                            skr_ref, wk_ref, wvt_ref, c_ref, kr_ref, q_ref, k_ref, vt_ref):
    q_heads, c_new, kr_new = _mla_common(qa_ref, kva_ref, kra_ref, qg_ref, kvg_ref, wuq_ref, cq_ref, sq_ref,
                                         ckr_ref, skr_ref)
    c_ref[...] = c_new
    kr_ref[...] = kr_new[:, :MLA_ROPE]
    for h in range(MLA_H):
        q_ref[:, h * HEAD_PAD:(h + 1) * HEAD_PAD] = (q_heads[h] * LOG2_E).astype(BF16)
    ckr = jnp.concatenate([c_new, kr_new], axis=1).astype(BF16)
    k_ref[...] = _dot(ckr, wk_ref[...]).astype(BF16)
    vt = _dot_nt(wvt_ref[...], ckr[:, :MLA_KV_LORA])
    row = lax.broadcasted_iota(jnp.int32, vt.shape, 0) % VT_ROWS
    vt_ref[...] = jnp.where(row == MLA_DV, 1.0, vt).astype(BF16)


def _mla_prep_sample_kernel(qa_ref, kva_ref, kra_ref, qg_ref, kvg_ref, wuq_ref, cq_ref, sq_ref, ckr_ref,
                            skr_ref, wabs_ref, c_ref, kr_ref, q_ref):
    q_heads, c_new, kr_new = _mla_common(qa_ref, kva_ref, kra_ref, qg_ref, kvg_ref, wuq_ref, cq_ref, sq_ref,
                                         ckr_ref, skr_ref)
    c_ref[...] = c_new
    kr_ref[...] = kr_new[:, :MLA_ROPE]
    for h in range(MLA_H):
        q_ref[:, h * QABS_W:(h + 1) * QABS_W] = _dot(q_heads[h].astype(BF16), wabs_ref[h]).astype(BF16)


def _mla_prep(z, tabs, qg, kvg, wuq_p, extra, prompt):
    rows = z.shape[0]
    tm = min(512, rows)
    cq, sq, ckr, skr = tabs
    zc = lambda c, w: pl.BlockSpec((tm, w), lambda i, c=c, w=w: (i, c // w))
    tab = pl.BlockSpec((tm, LANES), lambda i: (i, 0))
    full = lambda shape: pl.BlockSpec(shape, lambda i: (0,) * len(shape))
    in_specs = [zc(Z_QA, MLA_Q_LORA), zc(Z_KVA, MLA_KV_LORA), zc(Z_KRA, LANES), full((1, MLA_Q_LORA)),
                full((1, MLA_KV_LORA)), full(wuq_p.shape), tab, tab, tab, tab] + [full(e.shape) for e in extra]
    rowblk = lambda w: pl.BlockSpec((tm, w), lambda i: (i, 0))
    out_specs = [rowblk(MLA_KV_LORA), rowblk(MLA_ROPE)]
    out_shape = [jax.ShapeDtypeStruct((rows, MLA_KV_LORA), F32), jax.ShapeDtypeStruct((rows, MLA_ROPE), F32)]
    if prompt:
        widths = [MLA_H * HEAD_PAD] * 2
        body = _mla_prep_prompt_kernel
    else:
        widths = [MLA_H * QABS_W]
        body = _mla_prep_sample_kernel
    out_specs += [rowblk(w) for w in widths]
    out_shape += [jax.ShapeDtypeStruct((rows, w), BF16) for w in widths]
    if prompt:
        out_specs.append(pl.BlockSpec((MLA_H * VT_ROWS, tm), lambda i: (0, i)))
        out_shape.append(jax.ShapeDtypeStruct((MLA_H * VT_ROWS, rows), BF16))
    return pl.pallas_call(
        body,
        grid=(rows // tm,),
        in_specs=in_specs,
        out_specs=out_specs,
        out_shape=out_shape,
        compiler_params=_params(("parallel",)),
        name="mla_prep_prompt" if prompt else "mla_prep_sample",
    )(z, z, z, qg.reshape(1, -1), kvg.reshape(1, -1), wuq_p, cq, sq, ckr, skr, *extra)


def _flash_kernel(qi_ref, ki_ref, q_ref, k_ref, vt_ref, o_ref, m_ref, acc_ref):
    qi = qi_ref[pl.program_id(0)]
    ki = ki_ref[pl.program_id(0)]
    tq = q_ref.shape[0]
    tk = k_ref.shape[0]
    first_diag = qi * (tq // tk)

    @pl.when(ki == 0)
    def _():
        m_ref[...] = jnp.full_like(m_ref, -jnp.inf)
        acc_ref[...] = jnp.zeros_like(acc_ref)

    def update(diag):
        first_strip = 0 if diag is None else diag
        chains = [(h, c) for h in range(MLA_H) for c in range(first_strip, tq // FLASH_STRIP)]

        def scores(h, c):
            hs = slice(h * HEAD_PAD, (h + 1) * HEAD_PAD)
            return _dot_nt(k_ref[:, hs], q_ref[c * FLASH_STRIP:(c + 1) * FLASH_STRIP, hs])

        pending = [scores(*ch) for ch in chains[:FLASH_AHEAD]]
        for n, (h, c) in enumerate(chains):
            cs = slice(c * FLASH_STRIP, (c + 1) * FLASH_STRIP)
            s = pending.pop(0)
            if n + FLASH_AHEAD < len(chains):
                pending.append(scores(*chains[n + FLASH_AHEAD]))
            if c == diag:
                key = lax.broadcasted_iota(jnp.int32, s.shape, 0)
                query = lax.broadcasted_iota(jnp.int32, s.shape, 1)
                s = jnp.where(key <= query, s, -jnp.inf)
            m_old = m_ref[h, :, cs]
            m_new = jnp.maximum(m_old, jnp.max(s, axis=0, keepdims=True))
            p = jnp.exp2(s - m_new)
            alpha = jnp.exp2(m_old - m_new)
            pv = _dot(vt_ref[h * VT_ROWS:(h + 1) * VT_ROWS, :], p.astype(BF16))
            acc_ref[h, :, cs] = alpha * acc_ref[h, :, cs] + pv
            m_ref[h, :, cs] = m_new

    @pl.when(ki < first_diag)
    def _():
        update(None)

    for d in range(tq // tk):
        @pl.when(ki == first_diag + d)
        def _(d=d):
            update(d)

    @pl.when(ki == first_diag + tq // tk - 1)
    def _():
        for h in range(MLA_H):
            acc = acc_ref[h]
            o_ref[:, h * MLA_DV:(h + 1) * MLA_DV] = (acc[:MLA_DV] / acc[MLA_DV:MLA_DV + 1]).T


def _flash(q, k, vt):
    rows = q.shape[0]
    tq = min(FLASH_TQ, rows)
    tk = FLASH_STRIP
    assert rows % tq == 0 and tq % tk == 0
    w = MLA_H * HEAD_PAD
    pairs = [(i, j) for i in range(rows // tq) for j in range((i + 1) * (tq // tk))]
    qi_tab = jnp.asarray([p[0] for p in pairs], jnp.int32)
    ki_tab = jnp.asarray([p[1] for p in pairs], jnp.int32)
    grid_spec = pltpu.PrefetchScalarGridSpec(
        num_scalar_prefetch=2,
        grid=(len(pairs),),
        in_specs=[pl.BlockSpec((tq, w), lambda s, qi, ki: (qi[s], 0)),
                  pl.BlockSpec((tk, w), lambda s, qi, ki: (ki[s], 0)),
                  pl.BlockSpec((MLA_H * VT_ROWS, tk), lambda s, qi, ki: (0, ki[s]))],
        out_specs=pl.BlockSpec((tq, MLA_W), lambda s, qi, ki: (qi[s], 0)),
        scratch_shapes=[pltpu.VMEM((MLA_H, 1, tq), F32), pltpu.VMEM((MLA_H, VT_ROWS, tq), F32)],
    )
    return pl.pallas_call(
        _flash_kernel,
        grid_spec=grid_spec,
        out_shape=jax.ShapeDtypeStruct((rows, MLA_W), F32),
        compiler_params=_params(("arbitrary",)),
        name="flash_prompt",
    )(qi_tab, ki_tab, q, k, vt)


def _paged_kernel(layer, pages, group, nch, pt_ref, q_ref, cn_ref, krn_ref, lat_hbm, krt_hbm, o_ref,
                  latbuf, krtbuf, lat_sem, krt_sem, m_ref, l_ref, acc_ref):
    s = pl.program_id(0)
    n_steps = pl.num_programs(0)
    ji = s % nch
    slot = s % PAGED_SLOTS

    def page_copies(step, dst_slot):
        out = []
        for i in range(pages):
            pid = pt_ref[step * pages + i]
            out.append(pltpu.make_async_copy(lat_hbm.at[layer, pid], latbuf.at[dst_slot, i], lat_sem.at[dst_slot]))
            out.append(pltpu.make_async_copy(krt_hbm.at[layer, pid], krtbuf.at[dst_slot, i], krt_sem.at[dst_slot]))
        return out

    def wait_slot(dst_slot):
        for i in range(pages):
            pltpu.make_async_copy(lat_hbm.at[layer, 0], latbuf.at[dst_slot, i], lat_sem.at[dst_slot]).wait()
            pltpu.make_async_copy(krt_hbm.at[layer, 0], krtbuf.at[dst_slot, i], krt_sem.at[dst_slot]).wait()

    @pl.when(s == 0)
    def _():
        for a in range(PAGED_AHEAD):
            for cp in page_copies(jnp.minimum(a, n_steps - 1), a):
                cp.start()

    wait_slot(slot)
    ahead_slot = (s + PAGED_AHEAD) % PAGED_SLOTS
    ahead_copies = page_copies(jnp.minimum(s + PAGED_AHEAD, n_steps - 1), ahead_slot)
    lat_refs = [latbuf.at[slot, i] for i in range(pages)]
    krt_refs = [krtbuf.at[slot, i] for i in range(pages)]

    @pl.when(ji == 0)
    def _():
        m_ref[...] = jnp.full_like(m_ref, -jnp.inf)
        l_ref[...] = jnp.zeros_like(l_ref)
        acc_ref[...] = jnp.zeros_like(acc_ref)

    q_lat = q_ref[:, :MLA_KV_LORA]
    q_pe = q_ref[:, MLA_KV_LORA:MLA_KV_LORA + MLA_ROPE]

    def local_softmax(s, values):
        m = jnp.max(s, axis=1, keepdims=True)
        p = jnp.exp(s - m)
        return m, jnp.sum(p, axis=1, keepdims=True), _dot(p.astype(BF16), values)

    def merge(parts):
        m_old = m_ref[...]
        m_new = m_old
        for m, _, _ in parts:
            m_new = jnp.maximum(m_new, m)
        alpha = jnp.exp(m_old - m_new)
        l_new = alpha * l_ref[...]
        acc = alpha * acc_ref[...]
        for m, l, o in parts:
            w = jnp.exp(m - m_new)
            l_new = l_new + w * l
            acc = acc + w * o
        m_ref[...] = m_new
        l_ref[...] = l_new
        acc_ref[...] = acc

    values, scores = [], []
    n_groups = pages // group
    per_group = len(ahead_copies) // n_groups
    for g in range(n_groups):
        ids = range(g * group, (g + 1) * group)
        c = jnp.concatenate([lat_refs[i][...].astype(BF16) for i in ids], axis=0)
        krt = jnp.concatenate([krt_refs[i][...].astype(BF16) for i in ids], axis=1)
        values.append(c)
        scores.append(_dot_nt(q_lat, c) + _dot(q_pe, krt))
        for cp in ahead_copies[g * per_group:(g + 1) * per_group]:
            cp.start()
    merge([local_softmax(s, c) for s, c in zip(scores, values)])

    @pl.when(ji == nch - 1)
    def _():
        c_new = cn_ref[...].astype(BF16)
        s_new = _dot_nt(q_lat, c_new) + _dot_nt(q_pe, krn_ref[...].astype(BF16))
        t_q = lax.broadcasted_iota(jnp.int32, s_new.shape, 0) % SUBLANES
        t_k = lax.broadcasted_iota(jnp.int32, s_new.shape, 1)
        merge([local_softmax(jnp.where(t_k <= t_q, s_new, -jnp.inf), c_new)])
        o_ref[...] = acc_ref[...] / l_ref[...]

    @pl.when(s == n_steps - 1)
    def _():
        for a in range(1, PAGED_AHEAD + 1):
            wait_slot((s + a) % PAGED_SLOTS)


def _paged_attention(layer, q, c_new, kr_new, cache_lat, cache_krt, page_table):
    batch, rows, _ = q.shape
    t_new = c_new.shape[1]
    n_pages = page_table.shape[1]
    page = cache_lat.shape[2]
    pages = min(32, n_pages)
    group = min(4, pages)
    nch = n_pages // pages

    per_b = lambda r, w: pl.BlockSpec((None, r, w), lambda s, pt: (s // nch, 0, 0))
    grid_spec = pltpu.PrefetchScalarGridSpec(
        num_scalar_prefetch=1,
        grid=(batch * nch,),
        in_specs=[per_b(rows, QABS_W), per_b(t_new, MLA_KV_LORA), per_b(t_new, MLA_ROPE),
                  pl.BlockSpec(memory_space=pl.ANY), pl.BlockSpec(memory_space=pl.ANY)],
        out_specs=per_b(rows, MLA_KV_LORA),
        scratch_shapes=[pltpu.VMEM((PAGED_SLOTS, pages, page, MLA_KV_LORA), F32),
                        pltpu.VMEM((PAGED_SLOTS, pages, MLA_ROPE, page), F32),
                        pltpu.SemaphoreType.DMA((PAGED_SLOTS,)), pltpu.SemaphoreType.DMA((PAGED_SLOTS,)),
                        pltpu.VMEM((rows, 1), F32), pltpu.VMEM((rows, 1), F32),
                        pltpu.VMEM((rows, MLA_KV_LORA), F32)],
    )
    return pl.pallas_call(
        functools.partial(_paged_kernel, layer, pages, group, nch),
        grid_spec=grid_spec,
        out_shape=jax.ShapeDtypeStruct((batch, rows, MLA_KV_LORA), F32),
        compiler_params=_params(("arbitrary",)),
        name="paged_sample",
    )(page_table.reshape(-1), q, c_new, kr_new, cache_lat, cache_krt)


def _uv_kernel(o_ref, w_ref, y_ref):
    nb = o_ref.shape[0]
    t = o_ref.shape[2]
    for h in range(MLA_H):
        o_h = o_ref[:, h].reshape(nb * t, MLA_KV_LORA).astype(BF16)
        y_ref[:, h * MLA_DV:(h + 1) * MLA_DV] = _dot(o_h, w_ref[h])


def _uv_proj(o_lat, w_uv_h):
    batch, _, t, _ = o_lat.shape
    return pl.pallas_call(
        _uv_kernel,
        grid=(1,),
        in_specs=[pl.BlockSpec(o_lat.shape, lambda i: (0, 0, 0, 0)),
                  pl.BlockSpec(w_uv_h.shape, lambda i: (0, 0, 0))],
        out_specs=pl.BlockSpec((batch * t, MLA_W), lambda i: (0, 0)),
        out_shape=jax.ShapeDtypeStruct((batch * t, MLA_W), F32),
        compiler_params=_params(("arbitrary",)),
        name="uv_proj",
    )(o_lat, w_uv_h)


def _out_proj_kernel(final, x_ref, ya_ref, yb_ref, oc_ref, gm_ref, w_ref, fg_ref, *out_refs):
    yc = oc_ref[...] * _silu(gm_ref[...])
    upd = _dot(ya_ref[...].astype(BF16), w_ref[0:LRU_W, :])
    upd = upd + _dot(yb_ref[...].astype(BF16), w_ref[LRU_W:LRU_W + RET_W, :])
    upd = upd + _dot(yc.astype(BF16), w_ref[LRU_W + RET_W:, :])
    x_new = x_ref[...] + upd
    out_refs[0][...] = x_new
    if final:
        out_refs[1][...] = _rms(x_new, fg_ref[...])


def _out_proj(x2d, y_a, y_b, o_c, z, w_out, final_g, final):
    rows = x2d.shape[0]
    tm = min(512, rows)
    rowblk = lambda w: pl.BlockSpec((tm, w), lambda i: (i, 0))
    full = lambda shape: pl.BlockSpec(shape, lambda i: (0,) * len(shape))
    n_out = 2 if final else 1
    outs = pl.pallas_call(
        functools.partial(_out_proj_kernel, final),
        grid=(rows // tm,),
        in_specs=[rowblk(D_MODEL), rowblk(LRU_W), rowblk(RET_W), rowblk(MLA_W),
                  pl.BlockSpec((tm, MLA_W), lambda i: (i, Z_GM // MLA_W)), full(w_out.shape), full((1, D_MODEL))],
        out_specs=[rowblk(D_MODEL)] * n_out,
        out_shape=[jax.ShapeDtypeStruct((rows, D_MODEL), F32)] * n_out,
        compiler_params=_params(("parallel",)),
        name="out_proj_final" if final else "out_proj",
    )(x2d, y_a, y_b, o_c, z, w_out, final_g.reshape(1, D_MODEL))
    return outs


def _pad_cols(w, width):
    return jnp.pad(w, ((0, 0), (0, width - w.shape[1])))


def _layout_w_in(w):
    splits = np.cumsum([0, LRU_W, LRU_W, RET_H * RET_DK, RET_H * RET_DK, RET_W, RET_W, MLA_Q_LORA, MLA_KV_LORA,
                        MLA_ROPE, MLA_W])
    seg = [w[:, splits[i]:splits[i + 1]] for i in range(10)]
    x_lru, g_lru, q_r, k_r, v_r, g_r, q_a, kv_a, kr_a, g_m = seg
    cols = [x_lru, g_lru, _pad_cols(q_r, RET_QK_PAD), _pad_cols(k_r, RET_QK_PAD), q_a, kv_a, v_r, g_r, g_m,
            _pad_cols(kr_a, LANES)]
    return jnp.concatenate(cols, axis=1).astype(BF16)


def _block_diag(w):
    n, c, d = w.shape
    eye = jnp.eye(n, dtype=w.dtype)
    return (eye[:, None, :, None] * w[:, :, None, :]).reshape(n * c, n * d)


def _layout_w_uq(w):
    per_head = w.reshape(MLA_Q_LORA, MLA_H, MLA_NOPE + MLA_ROPE)
    per_head = jnp.pad(per_head, ((0, 0), (0, 0), (0, HEAD_PAD - MLA_NOPE - MLA_ROPE)))
    return per_head.reshape(MLA_Q_LORA, MLA_H * HEAD_PAD).astype(BF16)


def _layout_w_key(w_uk):
    top = jnp.pad(w_uk, ((0, 0), (0, 0), (0, HEAD_PAD - MLA_NOPE)))
    rope_rows = jnp.zeros((LANES, MLA_H, HEAD_PAD), F32)
    eye = jnp.eye(MLA_ROPE, dtype=F32)
    rope_rows = rope_rows.at[:MLA_ROPE, :, MLA_NOPE:MLA_NOPE + MLA_ROPE].set(
        jnp.broadcast_to(eye[:, None, :], (MLA_ROPE, MLA_H, MLA_ROPE)))
    return jnp.concatenate([top, rope_rows], axis=0).reshape(MLA_KV_LORA + LANES, MLA_H * HEAD_PAD).astype(BF16)


def _layout_w_val_t(w_uv):
    w = jnp.pad(w_uv.transpose(1, 2, 0), ((0, 0), (0, VT_ROWS - MLA_DV), (0, 0)))
    return w.reshape(MLA_H * VT_ROWS, MLA_KV_LORA).astype(BF16)


def _layout_w_abs(w_uk):
    w = jnp.zeros((MLA_H, HEAD_PAD, QABS_W), F32)
    w = w.at[:, :MLA_NOPE, :MLA_KV_LORA].set(w_uk.transpose(1, 2, 0))
    w = w.at[:, MLA_NOPE:MLA_NOPE + MLA_ROPE, MLA_KV_LORA:MLA_KV_LORA + MLA_ROPE].set(
        jnp.broadcast_to(jnp.eye(MLA_ROPE, dtype=F32), (MLA_H, MLA_ROPE, MLA_ROPE)))
    return w.astype(BF16)


def _rope_tables(pos):
    half = MLA_ROPE // 2
    inv = ROPE_BASE ** (-jnp.arange(half, dtype=F32) / half)
    ang = pos[:, None] * inv[None, :]
    cos = jnp.cos(ang)
    sin = jnp.sin(ang)
    c32 = jnp.concatenate([cos, cos], axis=1)
    s32 = jnp.concatenate([-sin, sin], axis=1)
    t = pos.shape[0]
    ret_c = _pad_cols(jnp.tile(c32, (1, RET_H)), RET_QK_PAD)
    ret_s = _pad_cols(jnp.tile(s32, (1, RET_H)), RET_QK_PAD)
    q_c = jnp.concatenate([jnp.ones((t, MLA_NOPE), F32), c32, jnp.zeros((t, HEAD_PAD - MLA_NOPE - MLA_ROPE), F32)], axis=1)
    q_s = jnp.concatenate([jnp.zeros((t, MLA_NOPE), F32), s32, jnp.zeros((t, HEAD_PAD - MLA_NOPE - MLA_ROPE), F32)], axis=1)
    kr_c = _pad_cols(c32, LANES)
    kr_s = _pad_cols(s32, LANES)
    return (ret_c, ret_s), (q_c, q_s, kr_c, kr_s)


def kernel(x_prompt, x_sample, cache_mla_latent, cache_mla_krope, state_ret, state_lru_h, state_conv, page_table,
           norm_g, w_in, conv_w, conv_b, lru_wa, lru_ba, lru_wx, lru_bx, lru_lambda, ret_gn_g, q_norm_g, w_uq,
           kv_norm_g, w_uk, w_uv, w_out, final_norm_g):
    bp, tp, _ = x_prompt.shape
    bs, ts, _ = x_sample.shape
    depth = w_in.shape[0]
    past_len = page_table.shape[1] * cache_mla_latent.shape[2]
    assert bp == 1 and ts == SUBLANES

    pos_p = jnp.arange(tp, dtype=F32)
    pos_s = past_len + jnp.arange(ts, dtype=F32)
    ret_tab_p, mla_tab_p = _rope_tables(pos_p)
    ret_tab_s, mla_tab_s = _rope_tables(pos_s)
    ret_const_p = _retention_consts(tp)
    ret_const_s = _retention_consts(ts)

    cache_krt = jnp.swapaxes(cache_mla_krope, 2, 3)

    xp = x_prompt.reshape(bp * tp, D_MODEL)
    xs = x_sample.reshape(bs * ts, D_MODEL)
    yp = ys = None
    outs = [[] for _ in range(10)]
    for l in range(depth):
        final = l == depth - 1
        w_in_p = _layout_w_in(w_in[l])
        lw = (conv_w[l], conv_b[l].reshape(1, -1), _block_diag(lru_wa[l]).astype(BF16), lru_ba[l].reshape(1, -1),
              _block_diag(lru_wx[l]).astype(BF16), lru_bx[l].reshape(1, -1), lru_lambda[l].reshape(1, -1))
        wuq_p = _layout_w_uq(w_uq[l])
        w_out_b = w_out[l].astype(BF16)

        z = _in_proj(xp, norm_g[l], w_in_p)
        y_a, h_last = _lru_prompt(z, lw)
        y_b, s_last = _retention(z, *ret_tab_p, jnp.zeros((bp, RET_H, RET_DK, RET_DV), F32), ret_const_p,
                                 ret_gn_g[l], bp, tp)
        c_new, kr_new, q, k, vt = _mla_prep(z, mla_tab_p, q_norm_g[l], kv_norm_g[l], wuq_p,
                                            (_layout_w_key(w_uk[l]), _layout_w_val_t(w_uv[l])), True)
        o_c = _flash(q, k, vt)
        res = _out_proj(xp, y_a, y_b, o_c, z, w_out_b, final_norm_g, final)
        xp = res[0]
        if final:
            yp = res[1]
        conv_rows = z[:, Z_XLRU:Z_XLRU + LRU_W].reshape(bp, tp, LRU_W)[:, tp - (CONV_K - 1):, :]
        for lst, val in zip(outs[:5], (c_new.reshape(bp, tp, -1), kr_new.reshape(bp, tp, -1), s_last,
                                       h_last.reshape(bp, LRU_W), conv_rows)):
            lst.append(val)

        z = _in_proj(xs, norm_g[l], w_in_p)
        buf = jnp.pad(state_conv[l], ((0, 0), (SUBLANES - (CONV_K - 1), 0), (0, 0)))
        xprev = jnp.roll(buf, -1, axis=0).reshape(bs * ts, LRU_W)
        y_a, h_last = _lru_sample(z, xprev, state_lru_h[l], lw)
        y_b, s_last = _retention(z, *ret_tab_s, state_ret[l], ret_const_s, ret_gn_g[l], bs, ts)
        tabs_s = tuple(jnp.tile(t, (bs, 1)) for t in mla_tab_s)
        c_new, kr_new, q_abs = _mla_prep(z, tabs_s, q_norm_g[l], kv_norm_g[l], wuq_p, (_layout_w_abs(w_uk[l]),), False)
        q_abs = q_abs.reshape(bs, ts, MLA_H, QABS_W).transpose(0, 2, 1, 3).reshape(bs, MLA_H * ts, QABS_W)
        o_lat = _paged_attention(l, q_abs, c_new.reshape(bs, ts, -1), kr_new.reshape(bs, ts, -1),
                                 cache_mla_latent, cache_krt, page_table)
        o_c = _uv_proj(o_lat.reshape(bs, MLA_H, ts, MLA_KV_LORA), w_uv[l].transpose(1, 0, 2).astype(BF16))
        res = _out_proj(xs, y_a, y_b, o_c, z, w_out_b, final_norm_g, final)
        xs = res[0]
        if final:
            ys = res[1]
        conv_rows = z[:, Z_XLRU:Z_XLRU + LRU_W].reshape(bs, ts, LRU_W)[:, ts - (CONV_K - 1):, :]
        for lst, val in zip(outs[5:], (c_new.reshape(bs, ts, -1), kr_new.reshape(bs, ts, -1), s_last, h_last,
                                       conv_rows)):
            lst.append(val)

    return (yp.reshape(bp, tp, D_MODEL), ys.reshape(bs, ts, D_MODEL)) + tuple(jnp.stack(o) for o in outs)
```

```python
import functools

import jax
import jax.numpy as jnp
import numpy as np
from jax import lax
from jax.experimental import pallas as pl
from jax.experimental.pallas import tpu as pltpu

F32 = jnp.float32
BF16 = jnp.bfloat16

D_MODEL = 1024
LRU_W = 256
LRU_BLOCKS = 4
CONV_K = 4
LRU_C = 8.0
RET_H = 6
RET_DK = 32
RET_DV = 64
RET_W = RET_H * RET_DV
RET_CHUNK = 128
MLA_H = 6
MLA_NOPE = 64
MLA_ROPE = 32
MLA_DV = 64
MLA_W = MLA_H * MLA_DV
MLA_Q_LORA = 256
MLA_KV_LORA = 256
ROPE_BASE = 10000.0
NORM_EPS = 1e-6
LOG2_E = 1.4426950408889634

LANES = 128
SUBLANES = 8
HEAD_PAD = 128
BF16_ROWS = 16
VT_ROWS = MLA_DV + BF16_ROWS
QABS_W = 384
RET_CHUNKS_PER_STEP = 4
RET_SEQS_PER_STEP = 8
PAGED_AHEAD = 2
PAGED_SLOTS = PAGED_AHEAD + 1
FLASH_STRIP = 512
FLASH_TQ = 1024
FLASH_AHEAD = 2
VMEM_LIMIT = 48 * 1024 * 1024

Z_XLRU, Z_GLRU, Z_QR, Z_KR, Z_QA, Z_KVA = 0, 256, 512, 768, 1024, 1280
Z_VR, Z_GR, Z_GM, Z_KRA = 1536, 1920, 2304, 2688
Z_W = 2816
RET_QK_PAD = 256


def _params(sem):
    return pltpu.CompilerParams(dimension_semantics=sem, vmem_limit_bytes=VMEM_LIMIT)


def _rms(x, g):
    return x * lax.rsqrt(jnp.mean(x * x, axis=-1, keepdims=True) + NORM_EPS) * g


def _silu(x):
    return x * jax.nn.sigmoid(x)


def _rope(x, c, s):
    n = x.shape[-1]
    lane = lax.broadcasted_iota(jnp.int32, x.shape, x.ndim - 1)
    up = pltpu.roll(x, n - MLA_ROPE // 2, axis=x.ndim - 1)
    dn = pltpu.roll(x, MLA_ROPE // 2, axis=x.ndim - 1)
    partner = jnp.where((lane % MLA_ROPE) < MLA_ROPE // 2, up, dn)
    return x * c + partner * s


def _dot(a, b):
    return jnp.dot(a, b, preferred_element_type=F32)


def _dot_nt(a, b):
    return lax.dot_general(a, b, (((1,), (1,)), ((), ())), preferred_element_type=F32)


def _dot_tn(a, b):
    return lax.dot_general(a, b, (((0,), (0,)), ((), ())), preferred_element_type=F32)


def _in_proj_kernel(x_ref, g_ref, w_ref, z_ref):
    h = _rms(x_ref[...], g_ref[...])
    z_ref[...] = _dot(h.astype(BF16), w_ref[...])


def _in_proj(x2d, g, w_p):
    rows = x2d.shape[0]
    tm = min(512, rows)
    return pl.pallas_call(
        _in_proj_kernel,
        grid=(rows // tm,),
        in_specs=[
            pl.BlockSpec((tm, D_MODEL), lambda i: (i, 0)),
            pl.BlockSpec((1, D_MODEL), lambda i: (0, 0)),
            pl.BlockSpec((D_MODEL, Z_W), lambda i: (0, 0)),
        ],
        out_specs=pl.BlockSpec((tm, Z_W), lambda i: (i, 0)),
        out_shape=jax.ShapeDtypeStruct((rows, Z_W), F32),
        compiler_params=_params(("parallel",)),
        name="in_proj",
    )(x2d, g.reshape(1, D_MODEL), w_p)


def _group_scan(a, u):
    row = lax.broadcasted_iota(jnp.int32, a.shape, 0) % SUBLANES
    d = 1
    while d < SUBLANES:
        keep = row >= d
        a_prev = jnp.where(keep, pltpu.roll(a, d, axis=0), 1.0)
        u_prev = jnp.where(keep, pltpu.roll(u, d, axis=0), 0.0)
        u = a * u_prev + u
        a = a * a_prev
        d *= 2
    return a, u


def _lru_gates(shifted, cw_ref, cb_ref, wa_ref, ba_ref, wx_ref, bx_ref, lam_ref):
    xc = cb_ref[...] + shifted[3] * cw_ref[0:1, :]
    xc = xc + shifted[2] * cw_ref[1:2, :]
    xc = xc + shifted[1] * cw_ref[2:3, :]
    xc = xc + shifted[0] * cw_ref[3:4, :]
    xcb = xc.astype(BF16)
    gate_r = jax.nn.sigmoid(_dot(xcb, wa_ref[...]) + ba_ref[...])
    gate_i = jax.nn.sigmoid(_dot(xcb, wx_ref[...]) + bx_ref[...])
    neg_lam = -lam_ref[...]
    softplus = jnp.maximum(neg_lam, 0.0) + jnp.log1p(jnp.exp(-jnp.abs(neg_lam)))
    log_a = -LRU_C * gate_r * softplus
    a = jnp.exp(log_a)
    u = jnp.sqrt(-jnp.tanh(log_a) * (a * a + 1.0)) * (gate_i * xc)
    return a, u


def _lru_prompt_kernel(x_ref, g_ref, cw_ref, cb_ref, wa_ref, ba_ref, wx_ref, bx_ref, lam_ref,
                       y_ref, hl_ref, xbuf_ref, a_ref, u_ref, h_ref):
    tm = x_ref.shape[0]

    @pl.when(pl.program_id(0) == 0)
    def _():
        xbuf_ref[0:SUBLANES, :] = jnp.zeros((SUBLANES, LRU_W), F32)
        h_ref[...] = jnp.zeros_like(h_ref)

    xbuf_ref[SUBLANES:, :] = x_ref[...]
    xb = xbuf_ref[...]
    shifted = [x_ref[...]] + [pltpu.roll(xb, d, axis=0)[SUBLANES:, :] for d in range(1, CONV_K)]
    xbuf_ref[0:SUBLANES, :] = x_ref[tm - SUBLANES:, :]

    a, u = _lru_gates(shifted, cw_ref, cb_ref, wa_ref, ba_ref, wx_ref, bx_ref, lam_ref)
    a, u = _group_scan(a, u)
    a_ref[...] = a
    u_ref[...] = u

    def body(g, h_prev):
        r0 = pl.multiple_of(g * SUBLANES, SUBLANES)
        h = u_ref[pl.ds(r0, SUBLANES), :] + a_ref[pl.ds(r0, SUBLANES), :] * h_prev
        u_ref[pl.ds(r0, SUBLANES), :] = h
        return jnp.broadcast_to(h[SUBLANES - 1:SUBLANES, :], (SUBLANES, LRU_W))

    h_last = lax.fori_loop(0, tm // SUBLANES, body, h_ref[...])
    h_ref[...] = h_last
    hl_ref[...] = h_last[0:1, :]
    y_ref[...] = u_ref[...] * _silu(g_ref[...])


def _lru_sample_kernel(x_ref, xprev_ref, g_ref, h0_ref, cw_ref, cb_ref, wa_ref, ba_ref, wx_ref, bx_ref,
                       lam_ref, y_ref, hs_ref):
    rows = x_ref.shape[0]
    x = x_ref[...]
    xprev = xprev_ref[...]
    row = lax.broadcasted_iota(jnp.int32, x.shape, 0) % SUBLANES
    shifted = [x] + [jnp.where(row >= d, pltpu.roll(x, d, axis=0), pltpu.roll(xprev, d, axis=0))
                     for d in range(1, CONV_K)]
    a, u = _lru_gates(shifted, cw_ref, cb_ref, wa_ref, ba_ref, wx_ref, bx_ref, lam_ref)
    a, u = _group_scan(a, u)
    nb = rows // SUBLANES
    h = u.reshape(nb, SUBLANES, LRU_W) + a.reshape(nb, SUBLANES, LRU_W) * h0_ref[...][:, None, :]
    h = h.reshape(rows, LRU_W)
    hs_ref[...] = h
    y_ref[...] = h * _silu(g_ref[...])


def _lru_weight_specs():
    full = lambda shape: pl.BlockSpec(shape, lambda *_: (0,) * len(shape))
    return [full((CONV_K, LRU_W)), full((1, LRU_W)), full((LRU_W, LRU_W)), full((1, LRU_W)),
            full((LRU_W, LRU_W)), full((1, LRU_W)), full((1, LRU_W))]


def _lru_prompt(z, lw):
    rows = z.shape[0]
    tm = min(512, rows)
    col = lambda c: pl.BlockSpec((tm, LRU_W), lambda i, c=c: (i, c // LRU_W))
    y, h_last = pl.pallas_call(
        _lru_prompt_kernel,
        grid=(rows // tm,),
        in_specs=[col(Z_XLRU), col(Z_GLRU)] + _lru_weight_specs(),
        out_specs=[pl.BlockSpec((tm, LRU_W), lambda i: (i, 0)), pl.BlockSpec((1, LRU_W), lambda i: (0, 0))],
        out_shape=[jax.ShapeDtypeStruct((rows, LRU_W), F32), jax.ShapeDtypeStruct((1, LRU_W), F32)],
        scratch_shapes=[pltpu.VMEM((tm + SUBLANES, LRU_W), F32), pltpu.VMEM((tm, LRU_W), F32),
                        pltpu.VMEM((tm, LRU_W), F32), pltpu.VMEM((SUBLANES, LRU_W), F32)],
        compiler_params=_params(("arbitrary",)),
        name="lru_prompt",
    )(z, z, *lw)
    return y, h_last


def _lru_sample(z, xprev, h0, lw):
    rows = z.shape[0]
    nb = rows // SUBLANES
    col = lambda c: pl.BlockSpec((rows, LRU_W), lambda i, c=c: (0, c // LRU_W))
    y, hs = pl.pallas_call(
        _lru_sample_kernel,
        grid=(1,),
        in_specs=[col(Z_XLRU), pl.BlockSpec((rows, LRU_W), lambda i: (0, 0)), col(Z_GLRU),
                  pl.BlockSpec((nb, LRU_W), lambda i: (0, 0))] + _lru_weight_specs(),
        out_specs=[pl.BlockSpec((rows, LRU_W), lambda i: (0, 0))] * 2,
        out_shape=[jax.ShapeDtypeStruct((rows, LRU_W), F32)] * 2,
        compiler_params=_params(("arbitrary",)),
        name="lru_sample",
    )(z, xprev, z, h0, *lw)
    return y, hs.reshape(nb, SUBLANES, LRU_W)[:, SUBLANES - 1, :]


def _retention_kernel(nb, q_ref, k_ref, v_ref, g_ref, c_ref, s_ref, s0_ref, intra_ref, qdec_ref, kdec_ref,
                      cdec_ref, gn_ref, y_ref, sl_ref, st_ref):
    ci = pl.program_id(1)
    chunk = c_ref.shape[0]

    @pl.when(ci == 0)
    def _():
        st_ref[...] = s0_ref[...]

    cos = c_ref[...]
    sin = s_ref[...]
    heads = range(RET_H)
    ks = [slice(h * RET_DK, (h + 1) * RET_DK) for h in heads]
    vs = [slice(h * RET_DV, (h + 1) * RET_DV) for h in heads]
    seqs = []
    for b in range(nb):
        rows = slice(b * chunk, (b + 1) * chunk)
        q = _rope(q_ref[rows, :], cos, sin)
        k = _rope(k_ref[rows, :], cos, sin) * (RET_DK ** -0.5)
        v = v_ref[rows, :]
        qd = q * qdec_ref[...]
        kd = k * kdec_ref[...]
        vh = [v[:, vs[h]].astype(BF16) for h in heads]
        scores = [_dot_nt(q[:, ks[h]].astype(BF16), k[:, ks[h]].astype(BF16)) for h in heads]
        states = [st_ref[b, h] for h in heads]
        cross = [_dot(qd[:, ks[h]].astype(BF16), states[h].astype(BF16)) for h in heads]
        for h in heads:
            st_ref[b, h] = cdec_ref[h] * states[h] + _dot_tn(kd[:, ks[h]].astype(BF16), vh[h])
        seqs.append((rows, vh, scores, cross))
    for rows, vh, scores, cross in seqs:
        gate = _silu(g_ref[rows, :])
        for h in heads:
            o = _dot((scores[h] * intra_ref[h]).astype(BF16), vh[h]) + cross[h]
            o = o * lax.rsqrt(jnp.mean(o * o, axis=-1, keepdims=True) + NORM_EPS) * gn_ref[:, vs[h]]
            y_ref[rows, vs[h]] = o * gate[:, vs[h]]

    @pl.when(ci == pl.num_programs(1) - 1)
    def _():
        sl_ref[...] = st_ref[...]


def _retention_wide_kernel(q_ref, k_ref, v_ref, g_ref, c_ref, s_ref, s0_ref, intra_ref, qdec_ref, kdec_ref,
                           dmat_ref, gmat_ref, gn_ref, y_ref, sl_ref, st_ref):
    ci = pl.program_id(1)
    heads = range(RET_H)

    @pl.when(ci == 0)
    def _():
        st_ref[...] = jnp.zeros_like(st_ref)
        for h in heads:
            st_ref[h * RET_DK:(h + 1) * RET_DK, h * RET_DV:(h + 1) * RET_DV] = s0_ref[h]

    chunk = intra_ref.shape[-1]
    n_sub = q_ref.shape[0] // chunk
    dmat = dmat_ref[...]
    subs = []
    for i in range(n_sub):
        rows = slice(i * chunk, (i + 1) * chunk)
        cos = c_ref[rows, :]
        sin = s_ref[rows, :]
        q = _rope(q_ref[rows, :], cos, sin)
        k = _rope(k_ref[rows, :], cos, sin) * (RET_DK ** -0.5)
        v = v_ref[rows, :]
        k_head = lax.broadcasted_iota(jnp.int32, k.shape, 1) // RET_DK
        qb = q.astype(BF16)
        scores = [_dot_nt(qb, jnp.where(k_head == h, k, 0.0).astype(BF16)) for h in heads]
        kv = _dot_tn((k * kdec_ref[...]).astype(BF16), v.astype(BF16))
        subs.append((rows, q, v, scores, kv))
    state = st_ref[...]
    crosses = []
    for rows, q, v, scores, kv in subs:
        crosses.append(_dot((q * qdec_ref[...]).astype(BF16), state.astype(BF16)))
        state = dmat * state + jnp.where(dmat != 0.0, kv, 0.0)
    st_ref[...] = state
    outs = []
    for (rows, q, v, scores, kv), cross in zip(subs, crosses):
        v_head = lax.broadcasted_iota(jnp.int32, v.shape, 1) // RET_DV
        p_cat = jnp.concatenate([(scores[h] * intra_ref[h]).astype(BF16) for h in heads], axis=1)
        v_rows = jnp.concatenate([jnp.where(v_head == h, v, 0.0).astype(BF16) for h in heads], axis=0)
        outs.append(_dot(p_cat, v_rows) + cross)
    for (rows, q, v, scores, kv), o in zip(subs, outs):
        sq = o * o
        sq_hi = sq.astype(BF16)
        sq_lo = (sq - sq_hi.astype(F32)).astype(BF16)
        mean_sq = _dot(sq_hi, gmat_ref[...]) + _dot(sq_lo, gmat_ref[...])
        y_ref[rows, :] = o * lax.rsqrt(mean_sq + NORM_EPS) * gn_ref[...] * _silu(g_ref[rows, :])

    @pl.when(ci == pl.num_programs(1) - 1)
    def _():
        for h in heads:
            sl_ref[h] = st_ref[h * RET_DK:(h + 1) * RET_DK, h * RET_DV:(h + 1) * RET_DV]


def _retention_wide(z, cos, sin, s0, consts, gn, batch, seq):
    intra, qdec, kdec, cdec = consts
    chunk = intra.shape[-1]
    nc = seq // chunk
    row_head = jnp.arange(RET_QK_PAD) // RET_DK
    col_head = jnp.arange(RET_W) // RET_DV
    same = row_head[:, None] == col_head[None, :]
    dmat = jnp.where(same, cdec[:, 0, 0][jnp.minimum(row_head, RET_H - 1)][:, None], 0.0).astype(F32)
    gmat = jnp.where(col_head[:, None] == col_head[None, :], 1.0 / RET_DV, 0.0).astype(BF16)
    n_sub = RET_CHUNKS_PER_STEP if nc % RET_CHUNKS_PER_STEP == 0 else 1
    rows = n_sub * chunk
    steps = nc // n_sub
    row = lambda b, c: b * steps + c
    zq = pl.BlockSpec((rows, RET_QK_PAD), lambda b, c: (row(b, c), Z_QR // RET_QK_PAD))
    zk = pl.BlockSpec((rows, RET_QK_PAD), lambda b, c: (row(b, c), Z_KR // RET_QK_PAD))
    zv = pl.BlockSpec((rows, RET_W), lambda b, c: (row(b, c), Z_VR // RET_W))
    zg = pl.BlockSpec((rows, RET_W), lambda b, c: (row(b, c), Z_GR // RET_W))
    tab = pl.BlockSpec((rows, RET_QK_PAD), lambda b, c: (c, 0))
    st = pl.BlockSpec((None, RET_H, RET_DK, RET_DV), lambda b, c: (b, 0, 0, 0))
    full = lambda shape: pl.BlockSpec(shape, lambda b, c: (0,) * len(shape))
    y, s_last = pl.pallas_call(
        _retention_wide_kernel,
        grid=(batch, steps),
        in_specs=[zq, zk, zv, zg, tab, tab, st, full(intra.shape), full(qdec.shape), full(kdec.shape),
                  full(dmat.shape), full(gmat.shape), full((1, RET_W))],
        out_specs=[pl.BlockSpec((rows, RET_W), lambda b, c: (row(b, c), 0)), st],
        out_shape=[jax.ShapeDtypeStruct((batch * seq, RET_W), F32),
                   jax.ShapeDtypeStruct((batch, RET_H, RET_DK, RET_DV), F32)],
        scratch_shapes=[pltpu.VMEM((RET_QK_PAD, RET_W), F32)],
        compiler_params=_params(("arbitrary", "arbitrary")),
        name="retention_wide",
    )(z, z, z, z, cos, sin, s0, intra, qdec, kdec, dmat, gmat, gn.reshape(1, RET_W))
    return y, s_last


def _retention(z, cos, sin, s0, consts, gn, batch, seq):
    intra, qdec, kdec, cdec = consts
    chunk = intra.shape[-1]
    nc = seq // chunk
    nb = RET_SEQS_PER_STEP if (nc == 1 and batch % RET_SEQS_PER_STEP == 0) else 1
    rows = nb * chunk
    row = lambda b, c: b * nc + c
    zq = pl.BlockSpec((rows, RET_QK_PAD), lambda b, c: (row(b, c), Z_QR // RET_QK_PAD))
    zk = pl.BlockSpec((rows, RET_QK_PAD), lambda b, c: (row(b, c), Z_KR // RET_QK_PAD))
    zv = pl.BlockSpec((rows, RET_W), lambda b, c: (row(b, c), Z_VR // RET_W))
    zg = pl.BlockSpec((rows, RET_W), lambda b, c: (row(b, c), Z_GR // RET_W))
    tab = pl.BlockSpec((chunk, RET_QK_PAD), lambda b, c: (c, 0))
    st = pl.BlockSpec((nb, RET_H, RET_DK, RET_DV), lambda b, c: (b, 0, 0, 0))
    full = lambda shape: pl.BlockSpec(shape, lambda b, c: (0,) * len(shape))
    y, s_last = pl.pallas_call(
        functools.partial(_retention_kernel, nb),
        grid=(batch // nb, nc),
        in_specs=[zq, zk, zv, zg, tab, tab, st, full(intra.shape), full(qdec.shape), full(kdec.shape),
                  full(cdec.shape), full((1, RET_W))],
        out_specs=[pl.BlockSpec((rows, RET_W), lambda b, c: (row(b, c), 0)), st],
        out_shape=[jax.ShapeDtypeStruct((batch * seq, RET_W), F32),
                   jax.ShapeDtypeStruct((batch, RET_H, RET_DK, RET_DV), F32)],
        scratch_shapes=[pltpu.VMEM((nb, RET_H, RET_DK, RET_DV), F32)],
        compiler_params=_params(("arbitrary", "arbitrary")),
        name="retention",
    )(z, z, z, z, cos, sin, s0, intra, qdec, kdec, cdec, gn.reshape(1, RET_W))
    return y, s_last


def _retention_consts(seq):
    chunk = min(RET_CHUNK, seq)
    log_g = jnp.log1p(-jnp.exp2(-5.0 - jnp.arange(RET_H, dtype=F32)))
    idx = jnp.arange(chunk, dtype=F32)
    diff = idx[:, None] - idx[None, :]
    intra = jnp.where(diff >= 0, jnp.exp(log_g[:, None, None] * jnp.maximum(diff, 0.0)), 0.0)
    q_dec = jnp.exp(log_g[:, None] * (idx + 1.0))
    k_dec = jnp.exp(log_g[:, None] * (chunk - 1.0 - idx))
    chunk_dec = jnp.exp(log_g * chunk)
    expand = lambda t: jnp.pad(jnp.repeat(t.T, RET_DK, axis=1), ((0, 0), (0, RET_QK_PAD - RET_H * RET_DK)))
    cdec = jnp.broadcast_to(chunk_dec[:, None, None], (RET_H, RET_DK, RET_DV))
    return intra, expand(q_dec), expand(k_dec), cdec


def _mla_common(qa_ref, kva_ref, kra_ref, qg_ref, kvg_ref, wuq_ref, cq_ref, sq_ref, ckr_ref, skr_ref):
    cq = _rms(qa_ref[...], qg_ref[...]).astype(BF16)
    qh = _dot(cq, wuq_ref[...])
    cos = cq_ref[...]
    sin = sq_ref[...]
    scale = (MLA_NOPE + MLA_ROPE) ** -0.5
    q_heads = [_rope(qh[:, h * HEAD_PAD:(h + 1) * HEAD_PAD], cos, sin) * scale for h in range(MLA_H)]
    c_new = _rms(kva_ref[...], kvg_ref[...])
    kr_new = _rope(kra_ref[...], ckr_ref[...], skr_ref[...])
    return q_heads, c_new, kr_new


def _mla_prep_prompt_kernel(qa_ref, kva_ref, kra_ref, qg_ref, kvg_ref, wuq_ref, cq_ref, sq_ref, ckr_ref,
                            skr_ref, wk_ref, wvt_ref, c_ref, kr_ref, q_ref, k_ref, vt_ref):
    q_heads, c_new, kr_new = _mla_common(qa_ref, kva_ref, kra_ref, qg_ref, kvg_ref, wuq_ref, cq_ref, sq_ref,
                                         ckr_ref, skr_ref)
    c_ref[...] = c_new
    kr_ref[...] = kr_new[:, :MLA_ROPE]
    for h in range(MLA_H):
        q_ref[:, h * HEAD_PAD:(h + 1) * HEAD_PAD] = (q_heads[h] * LOG2_E).astype(BF16)
    ckr = jnp.concatenate([c_new, kr_new], axis=1).astype(BF16)
    k_ref[...] = _dot(ckr, wk_ref[...]).astype(BF16)
    vt = _dot_nt(wvt_ref[...], ckr[:, :MLA_KV_LORA])
    row = lax.broadcasted_iota(jnp.int32, vt.shape, 0) % VT_ROWS
    vt_ref[...] = jnp.where(row == MLA_DV, 1.0, vt).astype(BF16)


def _mla_prep_sample_kernel(qa_ref, kva_ref, kra_ref, qg_ref, kvg_ref, wuq_ref, cq_ref, sq_ref, ckr_ref,
                            skr_ref, wabs_ref, c_ref, kr_ref, q_ref):
    q_heads, c_new, kr_new = _mla_common(qa_ref, kva_ref, kra_ref, qg_ref, kvg_ref, wuq_ref, cq_ref, sq_ref,
                                         ckr_ref, skr_ref)
    c_ref[...] = c_new
    kr_ref[...] = kr_new[:, :MLA_ROPE]
    for h in range(MLA_H):
        q_ref[:, h * QABS_W:(h + 1) * QABS_W] = _dot(q_heads[h].astype(BF16), wabs_ref[h]).astype(BF16)


def _mla_prep(z, tabs, qg, kvg, wuq_p, extra, prompt):
    rows = z.shape[0]
    tm = min(512, rows)
    cq, sq, ckr, skr = tabs
    zc = lambda c, w: pl.BlockSpec((tm, w), lambda i, c=c, w=w: (i, c // w))
    tab = pl.BlockSpec((tm, LANES), lambda i: (i, 0))
    full = lambda shape: pl.BlockSpec(shape, lambda i: (0,) * len(shape))
    in_specs = [zc(Z_QA, MLA_Q_LORA), zc(Z_KVA, MLA_KV_LORA), zc(Z_KRA, LANES), full((1, MLA_Q_LORA)),
                full((1, MLA_KV_LORA)), full(wuq_p.shape), tab, tab, tab, tab] + [full(e.shape) for e in extra]
    rowblk = lambda w: pl.BlockSpec((tm, w), lambda i: (i, 0))
    out_specs = [rowblk(MLA_KV_LORA), rowblk(MLA_ROPE)]
    out_shape = [jax.ShapeDtypeStruct((rows, MLA_KV_LORA), F32), jax.ShapeDtypeStruct((rows, MLA_ROPE), F32)]
    if prompt:
        widths = [MLA_H * HEAD_PAD] * 2
        body = _mla_prep_prompt_kernel
    else:
        widths = [MLA_H * QABS_W]
        body = _mla_prep_sample_kernel
    out_specs += [rowblk(w) for w in widths]
    out_shape += [jax.ShapeDtypeStruct((rows, w), BF16) for w in widths]
    if prompt:
        out_specs.append(pl.BlockSpec((MLA_H * VT_ROWS, tm), lambda i: (0, i)))
        out_shape.append(jax.ShapeDtypeStruct((MLA_H * VT_ROWS, rows), BF16))
    return pl.pallas_call(
        body,
        grid=(rows // tm,),
        in_specs=in_specs,
        out_specs=out_specs,
        out_shape=out_shape,
        compiler_params=_params(("parallel",)),
        name="mla_prep_prompt" if prompt else "mla_prep_sample",
    )(z, z, z, qg.reshape(1, -1), kvg.reshape(1, -1), wuq_p, cq, sq, ckr, skr, *extra)


def _flash_kernel(qi_ref, ki_ref, q_ref, k_ref, vt_ref, o_ref, m_ref, acc_ref):
    qi = qi_ref[pl.program_id(0)]
    ki = ki_ref[pl.program_id(0)]
    tq = q_ref.shape[0]
    tk = k_ref.shape[0]
    first_diag = qi * (tq // tk)

    @pl.when(ki == 0)
    def _():
        m_ref[...] = jnp.full_like(m_ref, -jnp.inf)
        acc_ref[...] = jnp.zeros_like(acc_ref)

    def update(diag):
        first_strip = 0 if diag is None else diag
        chains = [(h, c) for h in range(MLA_H) for c in range(first_strip, tq // FLASH_STRIP)]

        def scores(h, c):
            hs = slice(h * HEAD_PAD, (h + 1) * HEAD_PAD)
            return _dot_nt(k_ref[:, hs], q_ref[c * FLASH_STRIP:(c + 1) * FLASH_STRIP, hs])

        pending = [scores(*ch) for ch in chains[:FLASH_AHEAD]]
        for n, (h, c) in enumerate(chains):
            cs = slice(c * FLASH_STRIP, (c + 1) * FLASH_STRIP)
            s = pending.pop(0)
            if n + FLASH_AHEAD < len(chains):
                pending.append(scores(*chains[n + FLASH_AHEAD]))
            if c == diag:
                key = lax.broadcasted_iota(jnp.int32, s.shape, 0)
                query = lax.broadcasted_iota(jnp.int32, s.shape, 1)
                s = jnp.where(key <= query, s, -jnp.inf)
            m_old = m_ref[h, :, cs]
            m_new = jnp.maximum(m_old, jnp.max(s, axis=0, keepdims=True))
            p = jnp.exp2(s - m_new)
            alpha = jnp.exp2(m_old - m_new)
            pv = _dot(vt_ref[h * VT_ROWS:(h + 1) * VT_ROWS, :], p.astype(BF16))
            acc_ref[h, :, cs] = alpha * acc_ref[h, :, cs] + pv
            m_ref[h, :, cs] = m_new

    @pl.when(ki < first_diag)
    def _():
        update(None)

    for d in range(tq // tk):
        @pl.when(ki == first_diag + d)
        def _(d=d):
            update(d)

    @pl.when(ki == first_diag + tq // tk - 1)
    def _():
        for h in range(MLA_H):
            acc = acc_ref[h]
            o_ref[:, h * MLA_DV:(h + 1) * MLA_DV] = (acc[:MLA_DV] / acc[MLA_DV:MLA_DV + 1]).T


def _flash(q, k, vt):
    rows = q.shape[0]
    tq = min(FLASH_TQ, rows)
    tk = FLASH_STRIP
    assert rows % tq == 0 and tq % tk == 0
    w = MLA_H * HEAD_PAD
    pairs = [(i, j) for i in range(rows // tq) for j in range((i + 1) * (tq // tk))]
    qi_tab = jnp.asarray([p[0] for p in pairs], jnp.int32)
    ki_tab = jnp.asarray([p[1] for p in pairs], jnp.int32)
    grid_spec = pltpu.PrefetchScalarGridSpec(
        num_scalar_prefetch=2,
        grid=(len(pairs),),
        in_specs=[pl.BlockSpec((tq, w), lambda s, qi, ki: (qi[s], 0)),
                  pl.BlockSpec((tk, w), lambda s, qi, ki: (ki[s], 0)),
                  pl.BlockSpec((MLA_H * VT_ROWS, tk), lambda s, qi, ki: (0, ki[s]))],
        out_specs=pl.BlockSpec((tq, MLA_W), lambda s, qi, ki: (qi[s], 0)),
        scratch_shapes=[pltpu.VMEM((MLA_H, 1, tq), F32), pltpu.VMEM((MLA_H, VT_ROWS, tq), F32)],
    )
    return pl.pallas_call(
        _flash_kernel,
        grid_spec=grid_spec,
        out_shape=jax.ShapeDtypeStruct((rows, MLA_W), F32),
        compiler_params=_params(("arbitrary",)),
        name="flash_prompt",
    )(qi_tab, ki_tab, q, k, vt)


def _paged_kernel(layer, pages, group, nch, pt_ref, q_ref, cn_ref, krn_ref, lat_hbm, krt_hbm, o_ref,
                  latbuf, krtbuf, lat_sem, krt_sem, m_ref, l_ref, acc_ref):
    s = pl.program_id(0)
    n_steps = pl.num_programs(0)
    ji = s % nch
    slot = s % PAGED_SLOTS

    def page_copies(step, dst_slot):
        out = []
        for i in range(pages):
            pid = pt_ref[step * pages + i]
            out.append(pltpu.make_async_copy(lat_hbm.at[layer, pid], latbuf.at[dst_slot, i], lat_sem.at[dst_slot]))
            out.append(pltpu.make_async_copy(krt_hbm.at[layer, pid], krtbuf.at[dst_slot, i], krt_sem.at[dst_slot]))
        return out

    def wait_slot(dst_slot):
        for i in range(pages):
            pltpu.make_async_copy(lat_hbm.at[layer, 0], latbuf.at[dst_slot, i], lat_sem.at[dst_slot]).wait()
            pltpu.make_async_copy(krt_hbm.at[layer, 0], krtbuf.at[dst_slot, i], krt_sem.at[dst_slot]).wait()

    @pl.when(s == 0)
    def _():
        for a in range(PAGED_AHEAD):
            for cp in page_copies(jnp.minimum(a, n_steps - 1), a):
                cp.start()

    wait_slot(slot)
    ahead_slot = (s + PAGED_AHEAD) % PAGED_SLOTS
    ahead_copies = page_copies(jnp.minimum(s + PAGED_AHEAD, n_steps - 1), ahead_slot)
    lat_refs = [latbuf.at[slot, i] for i in range(pages)]
    krt_refs = [krtbuf.at[slot, i] for i in range(pages)]

    @pl.when(ji == 0)
    def _():
        m_ref[...] = jnp.full_like(m_ref, -jnp.inf)
        l_ref[...] = jnp.zeros_like(l_ref)
        acc_ref[...] = jnp.zeros_like(acc_ref)

    q_lat = q_ref[:, :MLA_KV_LORA]
    q_pe = q_ref[:, MLA_KV_LORA:MLA_KV_LORA + MLA_ROPE]

    def local_softmax(s, values):
        m = jnp.max(s, axis=1, keepdims=True)
        p = jnp.exp(s - m)
        return m, jnp.sum(p, axis=1, keepdims=True), _dot(p.astype(BF16), values)

    def merge(parts):
        m_old = m_ref[...]
        m_new = m_old
        for m, _, _ in parts:
            m_new = jnp.maximum(m_new, m)
        alpha = jnp.exp(m_old - m_new)
        l_new = alpha * l_ref[...]
        acc = alpha * acc_ref[...]
        for m, l, o in parts:
            w = jnp.exp(m - m_new)
            l_new = l_new + w * l
            acc = acc + w * o
        m_ref[...] = m_new
        l_ref[...] = l_new
        acc_ref[...] = acc

    values, scores = [], []
    n_groups = pages // group
    per_group = len(ahead_copies) // n_groups
    for g in range(n_groups):
        ids = range(g * group, (g + 1) * group)
        c = jnp.concatenate([lat_refs[i][...].astype(BF16) for i in ids], axis=0)
        krt = jnp.concatenate([krt_refs[i][...].astype(BF16) for i in ids], axis=1)
        values.append(c)
        scores.append(_dot_nt(q_lat, c) + _dot(q_pe, krt))
        for cp in ahead_copies[g * per_group:(g + 1) * per_group]:
            cp.start()
    merge([local_softmax(s, c) for s, c in zip(scores, values)])

    @pl.when(ji == nch - 1)
    def _():
        c_new = cn_ref[...].astype(BF16)
        s_new = _dot_nt(q_lat, c_new) + _dot_nt(q_pe, krn_ref[...].astype(BF16))
        t_q = lax.broadcasted_iota(jnp.int32, s_new.shape, 0) % SUBLANES
        t_k = lax.broadcasted_iota(jnp.int32, s_new.shape, 1)
        merge([local_softmax(jnp.where(t_k <= t_q, s_new, -jnp.inf), c_new)])
        o_ref[...] = acc_ref[...] / l_ref[...]

    @pl.when(s == n_steps - 1)
    def _():
        for a in range(1, PAGED_AHEAD + 1):
            wait_slot((s + a) % PAGED_SLOTS)


def _paged_attention(layer, q, c_new, kr_new, cache_lat, cache_krt, page_table):
    batch, rows, _ = q.shape
    t_new = c_new.shape[1]
    n_pages = page_table.shape[1]
    page = cache_lat.shape[2]
    pages = min(64, n_pages)
    group = min(4, pages)
    nch = n_pages // pages

    per_b = lambda r, w: pl.BlockSpec((None, r, w), lambda s, pt: (s // nch, 0, 0))
    grid_spec = pltpu.PrefetchScalarGridSpec(
        num_scalar_prefetch=1,
        grid=(batch * nch,),
        in_specs=[per_b(rows, QABS_W), per_b(t_new, MLA_KV_LORA), per_b(t_new, MLA_ROPE),
                  pl.BlockSpec(memory_space=pl.ANY), pl.BlockSpec(memory_space=pl.ANY)],
        out_specs=per_b(rows, MLA_KV_LORA),
        scratch_shapes=[pltpu.VMEM((PAGED_SLOTS, pages, page, MLA_KV_LORA), F32),
                        pltpu.VMEM((PAGED_SLOTS, pages, MLA_ROPE, page), F32),
                        pltpu.SemaphoreType.DMA((PAGED_SLOTS,)), pltpu.SemaphoreType.DMA((PAGED_SLOTS,)),
                        pltpu.VMEM((rows, 1), F32), pltpu.VMEM((rows, 1), F32),
                        pltpu.VMEM((rows, MLA_KV_LORA), F32)],
    )
    return pl.pallas_call(
        functools.partial(_paged_kernel, layer, pages, group, nch),
        grid_spec=grid_spec,
        out_shape=jax.ShapeDtypeStruct((batch, rows, MLA_KV_LORA), F32),
        compiler_params=_params(("arbitrary",)),
        name="paged_sample",
    )(page_table.reshape(-1), q, c_new, kr_new, cache_lat, cache_krt)


def _uv_kernel(o_ref, w_ref, y_ref):
    nb = o_ref.shape[0]
    t = o_ref.shape[2]
    for h in range(MLA_H):
        o_h = o_ref[:, h].reshape(nb * t, MLA_KV_LORA).astype(BF16)
        y_ref[:, h * MLA_DV:(h + 1) * MLA_DV] = _dot(o_h, w_ref[h])


def _uv_proj(o_lat, w_uv_h):
    batch, _, t, _ = o_lat.shape
    return pl.pallas_call(
        _uv_kernel,
        grid=(1,),
        in_specs=[pl.BlockSpec(o_lat.shape, lambda i: (0, 0, 0, 0)),
                  pl.BlockSpec(w_uv_h.shape, lambda i: (0, 0, 0))],
        out_specs=pl.BlockSpec((batch * t, MLA_W), lambda i: (0, 0)),
        out_shape=jax.ShapeDtypeStruct((batch * t, MLA_W), F32),
        compiler_params=_params(("arbitrary",)),
        name="uv_proj",
    )(o_lat, w_uv_h)


def _out_proj_kernel(final, x_ref, ya_ref, yb_ref, oc_ref, gm_ref, w_ref, fg_ref, *out_refs):
    yc = oc_ref[...] * _silu(gm_ref[...])
    upd = _dot(ya_ref[...].astype(BF16), w_ref[0:LRU_W, :])
    upd = upd + _dot(yb_ref[...].astype(BF16), w_ref[LRU_W:LRU_W + RET_W, :])
    upd = upd + _dot(yc.astype(BF16), w_ref[LRU_W + RET_W:, :])
    x_new = x_ref[...] + upd
    out_refs[0][...] = x_new
    if final:
        out_refs[1][...] = _rms(x_new, fg_ref[...])


def _out_proj(x2d, y_a, y_b, o_c, z, w_out, final_g, final):
    rows = x2d.shape[0]
    tm = min(512, rows)
    rowblk = lambda w: pl.BlockSpec((tm, w), lambda i: (i, 0))
    full = lambda shape: pl.BlockSpec(shape, lambda i: (0,) * len(shape))
    n_out = 2 if final else 1
    outs = pl.pallas_call(
        functools.partial(_out_proj_kernel, final),
        grid=(rows // tm,),
        in_specs=[rowblk(D_MODEL), rowblk(LRU_W), rowblk(RET_W), rowblk(MLA_W),
                  pl.BlockSpec((tm, MLA_W), lambda i: (i, Z_GM // MLA_W)), full(w_out.shape), full((1, D_MODEL))],
        out_specs=[rowblk(D_MODEL)] * n_out,
        out_shape=[jax.ShapeDtypeStruct((rows, D_MODEL), F32)] * n_out,
        compiler_params=_params(("parallel",)),
        name="out_proj_final" if final else "out_proj",
    )(x2d, y_a, y_b, o_c, z, w_out, final_g.reshape(1, D_MODEL))
    return outs


def _pad_cols(w, width):
    return jnp.pad(w, ((0, 0), (0, width - w.shape[1])))


def _layout_w_in(w):
    splits = np.cumsum([0, LRU_W, LRU_W, RET_H * RET_DK, RET_H * RET_DK, RET_W, RET_W, MLA_Q_LORA, MLA_KV_LORA,
                        MLA_ROPE, MLA_W])
    seg = [w[:, splits[i]:splits[i + 1]] for i in range(10)]
    x_lru, g_lru, q_r, k_r, v_r, g_r, q_a, kv_a, kr_a, g_m = seg
    cols = [x_lru, g_lru, _pad_cols(q_r, RET_QK_PAD), _pad_cols(k_r, RET_QK_PAD), q_a, kv_a, v_r, g_r, g_m,
            _pad_cols(kr_a, LANES)]
    return jnp.concatenate(cols, axis=1).astype(BF16)


def _block_diag(w):
    n, c, d = w.shape
    eye = jnp.eye(n, dtype=w.dtype)
    return (eye[:, None, :, None] * w[:, :, None, :]).reshape(n * c, n * d)


def _layout_w_uq(w):
    per_head = w.reshape(MLA_Q_LORA, MLA_H, MLA_NOPE + MLA_ROPE)
    per_head = jnp.pad(per_head, ((0, 0), (0, 0), (0, HEAD_PAD - MLA_NOPE - MLA_ROPE)))
    return per_head.reshape(MLA_Q_LORA, MLA_H * HEAD_PAD).astype(BF16)


def _layout_w_key(w_uk):
    top = jnp.pad(w_uk, ((0, 0), (0, 0), (0, HEAD_PAD - MLA_NOPE)))
    rope_rows = jnp.zeros((LANES, MLA_H, HEAD_PAD), F32)
    eye = jnp.eye(MLA_ROPE, dtype=F32)
    rope_rows = rope_rows.at[:MLA_ROPE, :, MLA_NOPE:MLA_NOPE + MLA_ROPE].set(
        jnp.broadcast_to(eye[:, None, :], (MLA_ROPE, MLA_H, MLA_ROPE)))
    return jnp.concatenate([top, rope_rows], axis=0).reshape(MLA_KV_LORA + LANES, MLA_H * HEAD_PAD).astype(BF16)


def _layout_w_val_t(w_uv):
    w = jnp.pad(w_uv.transpose(1, 2, 0), ((0, 0), (0, VT_ROWS - MLA_DV), (0, 0)))
    return w.reshape(MLA_H * VT_ROWS, MLA_KV_LORA).astype(BF16)


def _layout_w_abs(w_uk):
    w = jnp.zeros((MLA_H, HEAD_PAD, QABS_W), F32)
    w = w.at[:, :MLA_NOPE, :MLA_KV_LORA].set(w_uk.transpose(1, 2, 0))
    w = w.at[:, MLA_NOPE:MLA_NOPE + MLA_ROPE, MLA_KV_LORA:MLA_KV_LORA + MLA_ROPE].set(
        jnp.broadcast_to(jnp.eye(MLA_ROPE, dtype=F32), (MLA_H, MLA_ROPE, MLA_ROPE)))
    return w.astype(BF16)


def _rope_tables(pos):
    half = MLA_ROPE // 2
    inv = ROPE_BASE ** (-jnp.arange(half, dtype=F32) / half)
    ang = pos[:, None] * inv[None, :]
    cos = jnp.cos(ang)
    sin = jnp.sin(ang)
    c32 = jnp.concatenate([cos, cos], axis=1)
    s32 = jnp.concatenate([-sin, sin], axis=1)
    t = pos.shape[0]
    ret_c = _pad_cols(jnp.tile(c32, (1, RET_H)), RET_QK_PAD)
    ret_s = _pad_cols(jnp.tile(s32, (1, RET_H)), RET_QK_PAD)
    q_c = jnp.concatenate([jnp.ones((t, MLA_NOPE), F32), c32, jnp.zeros((t, HEAD_PAD - MLA_NOPE - MLA_ROPE), F32)], axis=1)
    q_s = jnp.concatenate([jnp.zeros((t, MLA_NOPE), F32), s32, jnp.zeros((t, HEAD_PAD - MLA_NOPE - MLA_ROPE), F32)], axis=1)
    kr_c = _pad_cols(c32, LANES)
    kr_s = _pad_cols(s32, LANES)
    return (ret_c, ret_s), (q_c, q_s, kr_c, kr_s)


def kernel(x_prompt, x_sample, cache_mla_latent, cache_mla_krope, state_ret, state_lru_h, state_conv, page_table,
           norm_g, w_in, conv_w, conv_b, lru_wa, lru_ba, lru_wx, lru_bx, lru_lambda, ret_gn_g, q_norm_g, w_uq,
           kv_norm_g, w_uk, w_uv, w_out, final_norm_g):
    bp, tp, _ = x_prompt.shape
    bs, ts, _ = x_sample.shape
    depth = w_in.shape[0]
    past_len = page_table.shape[1] * cache_mla_latent.shape[2]
    assert bp == 1 and ts == SUBLANES

    pos_p = jnp.arange(tp, dtype=F32)
    pos_s = past_len + jnp.arange(ts, dtype=F32)
    ret_tab_p, mla_tab_p = _rope_tables(pos_p)
    ret_tab_s, mla_tab_s = _rope_tables(pos_s)
    ret_const_p = _retention_consts(tp)
    ret_const_s = _retention_consts(ts)

    cache_krt = jnp.swapaxes(cache_mla_krope, 2, 3)

    xp = x_prompt.reshape(bp * tp, D_MODEL)
    xs = x_sample.reshape(bs * ts, D_MODEL)
    yp = ys = None
    outs = [[] for _ in range(10)]
    for l in range(depth):
        final = l == depth - 1
        w_in_p = _layout_w_in(w_in[l])
        lw = (conv_w[l], conv_b[l].reshape(1, -1), _block_diag(lru_wa[l]).astype(BF16), lru_ba[l].reshape(1, -1),
              _block_diag(lru_wx[l]).astype(BF16), lru_bx[l].reshape(1, -1), lru_lambda[l].reshape(1, -1))
        wuq_p = _layout_w_uq(w_uq[l])
        w_out_b = w_out[l].astype(BF16)

        z = _in_proj(xp, norm_g[l], w_in_p)
        y_a, h_last = _lru_prompt(z, lw)
        y_b, s_last = _retention_wide(z, *ret_tab_p, jnp.zeros((bp, RET_H, RET_DK, RET_DV), F32), ret_const_p,
                                      ret_gn_g[l], bp, tp)
        c_new, kr_new, q, k, vt = _mla_prep(z, mla_tab_p, q_norm_g[l], kv_norm_g[l], wuq_p,
                                            (_layout_w_key(w_uk[l]), _layout_w_val_t(w_uv[l])), True)
        o_c = _flash(q, k, vt)
        res = _out_proj(xp, y_a, y_b, o_c, z, w_out_b, final_norm_g, final)
        xp = res[0]
        if final:
            yp = res[1]
        conv_rows = z[:, Z_XLRU:Z_XLRU + LRU_W].reshape(bp, tp, LRU_W)[:, tp - (CONV_K - 1):, :]
        for lst, val in zip(outs[:5], (c_new.reshape(bp, tp, -1), kr_new.reshape(bp, tp, -1), s_last,
                                       h_last.reshape(bp, LRU_W), conv_rows)):
            lst.append(val)

        z = _in_proj(xs, norm_g[l], w_in_p)
        buf = jnp.pad(state_conv[l], ((0, 0), (SUBLANES - (CONV_K - 1), 0), (0, 0)))
        xprev = jnp.roll(buf, -1, axis=0).reshape(bs * ts, LRU_W)
        y_a, h_last = _lru_sample(z, xprev, state_lru_h[l], lw)
        y_b, s_last = _retention(z, *ret_tab_s, state_ret[l], ret_const_s, ret_gn_g[l], bs, ts)
        tabs_s = tuple(jnp.tile(t, (bs, 1)) for t in mla_tab_s)
        c_new, kr_new, q_abs = _mla_prep(z, tabs_s, q_norm_g[l], kv_norm_g[l], wuq_p, (_layout_w_abs(w_uk[l]),), False)
        q_abs = q_abs.reshape(bs, ts, MLA_H, QABS_W).transpose(0, 2, 1, 3).reshape(bs, MLA_H * ts, QABS_W)
        o_lat = _paged_attention(l, q_abs, c_new.reshape(bs, ts, -1), kr_new.reshape(bs, ts, -1),
                                 cache_mla_latent, cache_krt, page_table)
        o_c = _uv_proj(o_lat.reshape(bs, MLA_H, ts, MLA_KV_LORA), w_uv[l].transpose(1, 0, 2).astype(BF16))
        res = _out_proj(xs, y_a, y_b, o_c, z, w_out_b, final_norm_g, final)
        xs = res[0]
        if final:
            ys = res[1]
        conv_rows = z[:, Z_XLRU:Z_XLRU + LRU_W].reshape(bs, ts, LRU_W)[:, ts - (CONV_K - 1):, :]
        for lst, val in zip(outs[5:], (c_new.reshape(bs, ts, -1), kr_new.reshape(bs, ts, -1), s_last, h_last,
                                       conv_rows)):
            lst.append(val)

    return (yp.reshape(bp, tp, D_MODEL), ys.reshape(bs, ts, D_MODEL)) + tuple(jnp.stack(o) for o in outs)
```

```python
import functools

import jax
import jax.numpy as jnp
import numpy as np
from jax import lax
from jax.experimental import pallas as pl
from jax.experimental.pallas import tpu as pltpu

F32 = jnp.float32
BF16 = jnp.bfloat16

D_MODEL = 1024
LRU_W = 256
LRU_BLOCKS = 4
CONV_K = 4
LRU_C = 8.0
RET_H = 6
RET_DK = 32
RET_DV = 64
RET_W = RET_H * RET_DV
RET_CHUNK = 128
MLA_H = 6
MLA_NOPE = 64
MLA_ROPE = 32
MLA_DV = 64
MLA_W = MLA_H * MLA_DV
MLA_Q_LORA = 256
MLA_KV_LORA = 256
ROPE_BASE = 10000.0
NORM_EPS = 1e-6
LOG2_E = 1.4426950408889634

LANES = 128
SUBLANES = 8
HEAD_PAD = 128
BF16_ROWS = 16
VT_ROWS = MLA_DV + BF16_ROWS
QABS_W = 384
RET_CHUNKS_PER_STEP = 4
RET_SEQS_PER_STEP = 8
PAGED_AHEAD = 2
PAGED_SLOTS = PAGED_AHEAD + 1
FLASH_STRIP = 512
FLASH_TQ = 1024
FLASH_AHEAD = 2
VMEM_LIMIT = 48 * 1024 * 1024

Z_XLRU, Z_GLRU, Z_QR, Z_KR, Z_QA, Z_KVA = 0, 256, 512, 768, 1024, 1280
Z_VR, Z_GR, Z_GM, Z_KRA = 1536, 1920, 2304, 2688
Z_W = 2816
RET_QK_PAD = 256


def _params(sem):
    return pltpu.CompilerParams(dimension_semantics=sem, vmem_limit_bytes=VMEM_LIMIT)


def _rms(x, g):
    return x * lax.rsqrt(jnp.mean(x * x, axis=-1, keepdims=True) + NORM_EPS) * g


def _silu(x):
    return x * jax.nn.sigmoid(x)


def _rope(x, c, s):
    n = x.shape[-1]
    lane = lax.broadcasted_iota(jnp.int32, x.shape, x.ndim - 1)
    up = pltpu.roll(x, n - MLA_ROPE // 2, axis=x.ndim - 1)
    dn = pltpu.roll(x, MLA_ROPE // 2, axis=x.ndim - 1)
    partner = jnp.where((lane % MLA_ROPE) < MLA_ROPE // 2, up, dn)
    return x * c + partner * s


def _dot(a, b):
    return jnp.dot(a, b, preferred_element_type=F32)


def _dot_nt(a, b):
    return lax.dot_general(a, b, (((1,), (1,)), ((), ())), preferred_element_type=F32)


def _dot_tn(a, b):
    return lax.dot_general(a, b, (((0,), (0,)), ((), ())), preferred_element_type=F32)


def _in_proj_kernel(x_ref, g_ref, w_ref, z_ref):
    h = _rms(x_ref[...], g_ref[...])
    z_ref[...] = _dot(h.astype(BF16), w_ref[...])


def _in_proj(x2d, g, w_p):
    rows = x2d.shape[0]
    tm = min(512, rows)
    return pl.pallas_call(
        _in_proj_kernel,
        grid=(rows // tm,),
        in_specs=[
            pl.BlockSpec((tm, D_MODEL), lambda i: (i, 0)),
            pl.BlockSpec((1, D_MODEL), lambda i: (0, 0)),
            pl.BlockSpec((D_MODEL, Z_W), lambda i: (0, 0)),
        ],
        out_specs=pl.BlockSpec((tm, Z_W), lambda i: (i, 0)),
        out_shape=jax.ShapeDtypeStruct((rows, Z_W), F32),
        compiler_params=_params(("parallel",)),
        name="in_proj",
    )(x2d, g.reshape(1, D_MODEL), w_p)


def _group_scan(a, u):
    row = lax.broadcasted_iota(jnp.int32, a.shape, 0) % SUBLANES
    d = 1
    while d < SUBLANES:
        keep = row >= d
        a_prev = jnp.where(keep, pltpu.roll(a, d, axis=0), 1.0)
        u_prev = jnp.where(keep, pltpu.roll(u, d, axis=0), 0.0)
        u = a * u_prev + u
        a = a * a_prev
        d *= 2
    return a, u


def _lru_gates(shifted, cw_ref, cb_ref, wa_ref, ba_ref, wx_ref, bx_ref, lam_ref):
    xc = cb_ref[...] + shifted[3] * cw_ref[0:1, :]
    xc = xc + shifted[2] * cw_ref[1:2, :]
    xc = xc + shifted[1] * cw_ref[2:3, :]
    xc = xc + shifted[0] * cw_ref[3:4, :]
    xcb = xc.astype(BF16)
    gate_r = jax.nn.sigmoid(_dot(xcb, wa_ref[...]) + ba_ref[...])
    gate_i = jax.nn.sigmoid(_dot(xcb, wx_ref[...]) + bx_ref[...])
    neg_lam = -lam_ref[...]
    softplus = jnp.maximum(neg_lam, 0.0) + jnp.log1p(jnp.exp(-jnp.abs(neg_lam)))
    log_a = -LRU_C * gate_r * softplus
    a = jnp.exp(log_a)
    u = jnp.sqrt(-jnp.tanh(log_a) * (a * a + 1.0)) * (gate_i * xc)
    return a, u


def _lru_prompt_kernel(x_ref, g_ref, cw_ref, cb_ref, wa_ref, ba_ref, wx_ref, bx_ref, lam_ref,
                       y_ref, hl_ref, xbuf_ref, a_ref, u_ref, h_ref):
    tm = x_ref.shape[0]

    @pl.when(pl.program_id(0) == 0)
    def _():
        xbuf_ref[0:SUBLANES, :] = jnp.zeros((SUBLANES, LRU_W), F32)
        h_ref[...] = jnp.zeros_like(h_ref)

    xbuf_ref[SUBLANES:, :] = x_ref[...]
    xb = xbuf_ref[...]
    shifted = [x_ref[...]] + [pltpu.roll(xb, d, axis=0)[SUBLANES:, :] for d in range(1, CONV_K)]
    xbuf_ref[0:SUBLANES, :] = x_ref[tm - SUBLANES:, :]

    a, u = _lru_gates(shifted, cw_ref, cb_ref, wa_ref, ba_ref, wx_ref, bx_ref, lam_ref)
    a, u = _group_scan(a, u)
    a_ref[...] = a
    u_ref[...] = u

    def body(g, h_prev):
        r0 = pl.multiple_of(g * SUBLANES, SUBLANES)
        h = u_ref[pl.ds(r0, SUBLANES), :] + a_ref[pl.ds(r0, SUBLANES), :] * h_prev
        u_ref[pl.ds(r0, SUBLANES), :] = h
        return jnp.broadcast_to(h[SUBLANES - 1:SUBLANES, :], (SUBLANES, LRU_W))

    h_last = lax.fori_loop(0, tm // SUBLANES, body, h_ref[...])
    h_ref[...] = h_last
    hl_ref[...] = h_last[0:1, :]
    y_ref[...] = u_ref[...] * _silu(g_ref[...])


def _lru_sample_kernel(x_ref, xprev_ref, g_ref, h0_ref, cw_ref, cb_ref, wa_ref, ba_ref, wx_ref, bx_ref,
                       lam_ref, y_ref, hs_ref):
    rows = x_ref.shape[0]
    x = x_ref[...]
    xprev = xprev_ref[...]
    row = lax.broadcasted_iota(jnp.int32, x.shape, 0) % SUBLANES
    shifted = [x] + [jnp.where(row >= d, pltpu.roll(x, d, axis=0), pltpu.roll(xprev, d, axis=0))
                     for d in range(1, CONV_K)]
    a, u = _lru_gates(shifted, cw_ref, cb_ref, wa_ref, ba_ref, wx_ref, bx_ref, lam_ref)
    a, u = _group_scan(a, u)
    nb = rows // SUBLANES
    h = u.reshape(nb, SUBLANES, LRU_W) + a.reshape(nb, SUBLANES, LRU_W) * h0_ref[...][:, None, :]
    h = h.reshape(rows, LRU_W)
    hs_ref[...] = h
    y_ref[...] = h * _silu(g_ref[...])


def _lru_weight_specs():
    full = lambda shape: pl.BlockSpec(shape, lambda *_: (0,) * len(shape))
    return [full((CONV_K, LRU_W)), full((1, LRU_W)), full((LRU_W, LRU_W)), full((1, LRU_W)),
            full((LRU_W, LRU_W)), full((1, LRU_W)), full((1, LRU_W))]


def _lru_prompt(z, lw):
    rows = z.shape[0]
    tm = min(512, rows)
    col = lambda c: pl.BlockSpec((tm, LRU_W), lambda i, c=c: (i, c // LRU_W))
    y, h_last = pl.pallas_call(
        _lru_prompt_kernel,
        grid=(rows // tm,),
        in_specs=[col(Z_XLRU), col(Z_GLRU)] + _lru_weight_specs(),
        out_specs=[pl.BlockSpec((tm, LRU_W), lambda i: (i, 0)), pl.BlockSpec((1, LRU_W), lambda i: (0, 0))],
        out_shape=[jax.ShapeDtypeStruct((rows, LRU_W), F32), jax.ShapeDtypeStruct((1, LRU_W), F32)],
        scratch_shapes=[pltpu.VMEM((tm + SUBLANES, LRU_W), F32), pltpu.VMEM((tm, LRU_W), F32),
                        pltpu.VMEM((tm, LRU_W), F32), pltpu.VMEM((SUBLANES, LRU_W), F32)],
        compiler_params=_params(("arbitrary",)),
        name="lru_prompt",
    )(z, z, *lw)
    return y, h_last


def _lru_sample(z, xprev, h0, lw):
    rows = z.shape[0]
    nb = rows // SUBLANES
    col = lambda c: pl.BlockSpec((rows, LRU_W), lambda i, c=c: (0, c // LRU_W))
    y, hs = pl.pallas_call(
        _lru_sample_kernel,
        grid=(1,),
        in_specs=[col(Z_XLRU), pl.BlockSpec((rows, LRU_W), lambda i: (0, 0)), col(Z_GLRU),
                  pl.BlockSpec((nb, LRU_W), lambda i: (0, 0))] + _lru_weight_specs(),
        out_specs=[pl.BlockSpec((rows, LRU_W), lambda i: (0, 0))] * 2,
        out_shape=[jax.ShapeDtypeStruct((rows, LRU_W), F32)] * 2,
        compiler_params=_params(("arbitrary",)),
        name="lru_sample",
    )(z, xprev, z, h0, *lw)
    return y, hs.reshape(nb, SUBLANES, LRU_W)[:, SUBLANES - 1, :]


def _retention_kernel(nb, q_ref, k_ref, v_ref, g_ref, c_ref, s_ref, s0_ref, intra_ref, qdec_ref, kdec_ref,
                      cdec_ref, gn_ref, y_ref, sl_ref, st_ref):
    ci = pl.program_id(1)
    chunk = c_ref.shape[0]

    @pl.when(ci == 0)
    def _():
        st_ref[...] = s0_ref[...]

    cos = c_ref[...]
    sin = s_ref[...]
    heads = range(RET_H)
    ks = [slice(h * RET_DK, (h + 1) * RET_DK) for h in heads]
    vs = [slice(h * RET_DV, (h + 1) * RET_DV) for h in heads]
    seqs = []
    for b in range(nb):
        rows = slice(b * chunk, (b + 1) * chunk)
        q = _rope(q_ref[rows, :], cos, sin)
        k = _rope(k_ref[rows, :], cos, sin) * (RET_DK ** -0.5)
        v = v_ref[rows, :]
        qd = q * qdec_ref[...]
        kd = k * kdec_ref[...]
        vh = [v[:, vs[h]].astype(BF16) for h in heads]
        scores = [_dot_nt(q[:, ks[h]].astype(BF16), k[:, ks[h]].astype(BF16)) for h in heads]
        states = [st_ref[b, h] for h in heads]
        cross = [_dot(qd[:, ks[h]].astype(BF16), states[h].astype(BF16)) for h in heads]
        for h in heads:
            st_ref[b, h] = cdec_ref[h] * states[h] + _dot_tn(kd[:, ks[h]].astype(BF16), vh[h])
        seqs.append((rows, vh, scores, cross))
    for rows, vh, scores, cross in seqs:
        gate = _silu(g_ref[rows, :])
        for h in heads:
            o = _dot((scores[h] * intra_ref[h]).astype(BF16), vh[h]) + cross[h]
            o = o * lax.rsqrt(jnp.mean(o * o, axis=-1, keepdims=True) + NORM_EPS) * gn_ref[:, vs[h]]
            y_ref[rows, vs[h]] = o * gate[:, vs[h]]

    @pl.when(ci == pl.num_programs(1) - 1)
    def _():
        sl_ref[...] = st_ref[...]


def _retention_wide_kernel(q_ref, k_ref, v_ref, g_ref, c_ref, s_ref, s0_ref, intra_ref, qdec_ref, kdec_ref,
                           dmat_ref, gmat_ref, gn_ref, y_ref, sl_ref, st_ref):
    ci = pl.program_id(1)
    heads = range(RET_H)

    @pl.when(ci == 0)
    def _():
        st_ref[...] = jnp.zeros_like(st_ref)
        for h in heads:
            st_ref[h * RET_DK:(h + 1) * RET_DK, h * RET_DV:(h + 1) * RET_DV] = s0_ref[h]

    chunk = intra_ref.shape[-1]
    n_sub = q_ref.shape[0] // chunk
    dmat = dmat_ref[...]
    subs = []
    for i in range(n_sub):
        rows = slice(i * chunk, (i + 1) * chunk)
        cos = c_ref[rows, :]
        sin = s_ref[rows, :]
        q = _rope(q_ref[rows, :], cos, sin)
        k = _rope(k_ref[rows, :], cos, sin) * (RET_DK ** -0.5)
        v = v_ref[rows, :]
        k_head = lax.broadcasted_iota(jnp.int32, k.shape, 1) // RET_DK
        qb = q.astype(BF16)
        scores = [_dot_nt(qb, jnp.where(k_head == h, k, 0.0).astype(BF16)) for h in heads]
        kv = _dot_tn((k * kdec_ref[...]).astype(BF16), v.astype(BF16))
        subs.append((rows, q, v, scores, kv))
    state = st_ref[...]
    crosses = []
    for rows, q, v, scores, kv in subs:
        crosses.append(_dot((q * qdec_ref[...]).astype(BF16), state.astype(BF16)))
        state = dmat * state + jnp.where(dmat != 0.0, kv, 0.0)
    st_ref[...] = state
    outs = []
    for (rows, q, v, scores, kv), cross in zip(subs, crosses):
        v_head = lax.broadcasted_iota(jnp.int32, v.shape, 1) // RET_DV
        p_cat = jnp.concatenate([(scores[h] * intra_ref[h]).astype(BF16) for h in heads], axis=1)
        v_rows = jnp.concatenate([jnp.where(v_head == h, v, 0.0).astype(BF16) for h in heads], axis=0)
        outs.append(_dot(p_cat, v_rows) + cross)
    for (rows, q, v, scores, kv), o in zip(subs, outs):
        sq = o * o
        sq_hi = sq.astype(BF16)
        sq_lo = (sq - sq_hi.astype(F32)).astype(BF16)
        mean_sq = _dot(sq_hi, gmat_ref[...]) + _dot(sq_lo, gmat_ref[...])
        y_ref[rows, :] = o * lax.rsqrt(mean_sq + NORM_EPS) * gn_ref[...] * _silu(g_ref[rows, :])

    @pl.when(ci == pl.num_programs(1) - 1)
    def _():
        for h in heads:
            sl_ref[h] = st_ref[h * RET_DK:(h + 1) * RET_DK, h * RET_DV:(h + 1) * RET_DV]


def _retention_wide(z, cos, sin, s0, consts, gn, batch, seq):
    intra, qdec, kdec, cdec = consts
    chunk = intra.shape[-1]
    nc = seq // chunk
    row_head = jnp.arange(RET_QK_PAD) // RET_DK
    col_head = jnp.arange(RET_W) // RET_DV
    same = row_head[:, None] == col_head[None, :]
    dmat = jnp.where(same, cdec[:, 0, 0][jnp.minimum(row_head, RET_H - 1)][:, None], 0.0).astype(F32)
    gmat = jnp.where(col_head[:, None] == col_head[None, :], 1.0 / RET_DV, 0.0).astype(BF16)
    n_sub = RET_CHUNKS_PER_STEP if nc % RET_CHUNKS_PER_STEP == 0 else 1
    rows = n_sub * chunk
    steps = nc // n_sub
    row = lambda b, c: b * steps + c
    zq = pl.BlockSpec((rows, RET_QK_PAD), lambda b, c: (row(b, c), Z_QR // RET_QK_PAD))
    zk = pl.BlockSpec((rows, RET_QK_PAD), lambda b, c: (row(b, c), Z_KR // RET_QK_PAD))
    zv = pl.BlockSpec((rows, RET_W), lambda b, c: (row(b, c), Z_VR // RET_W))
    zg = pl.BlockSpec((rows, RET_W), lambda b, c: (row(b, c), Z_GR // RET_W))
    tab = pl.BlockSpec((rows, RET_QK_PAD), lambda b, c: (c, 0))
    st = pl.BlockSpec((None, RET_H, RET_DK, RET_DV), lambda b, c: (b, 0, 0, 0))
    full = lambda shape: pl.BlockSpec(shape, lambda b, c: (0,) * len(shape))
    y, s_last = pl.pallas_call(
        _retention_wide_kernel,
        grid=(batch, steps),
        in_specs=[zq, zk, zv, zg, tab, tab, st, full(intra.shape), full(qdec.shape), full(kdec.shape),
                  full(dmat.shape), full(gmat.shape), full((1, RET_W))],
        out_specs=[pl.BlockSpec((rows, RET_W), lambda b, c: (row(b, c), 0)), st],
        out_shape=[jax.ShapeDtypeStruct((batch * seq, RET_W), F32),
                   jax.ShapeDtypeStruct((batch, RET_H, RET_DK, RET_DV), F32)],
        scratch_shapes=[pltpu.VMEM((RET_QK_PAD, RET_W), F32)],
        compiler_params=_params(("arbitrary", "arbitrary")),
        name="retention_wide",
    )(z, z, z, z, cos, sin, s0, intra, qdec, kdec, dmat, gmat, gn.reshape(1, RET_W))
    return y, s_last


def _retention(z, cos, sin, s0, consts, gn, batch, seq):
    intra, qdec, kdec, cdec = consts
    chunk = intra.shape[-1]
    nc = seq // chunk
    nb = RET_SEQS_PER_STEP if (nc == 1 and batch % RET_SEQS_PER_STEP == 0) else 1
    rows = nb * chunk
    row = lambda b, c: b * nc + c
    zq = pl.BlockSpec((rows, RET_QK_PAD), lambda b, c: (row(b, c), Z_QR // RET_QK_PAD))
    zk = pl.BlockSpec((rows, RET_QK_PAD), lambda b, c: (row(b, c), Z_KR // RET_QK_PAD))
    zv = pl.BlockSpec((rows, RET_W), lambda b, c: (row(b, c), Z_VR // RET_W))
    zg = pl.BlockSpec((rows, RET_W), lambda b, c: (row(b, c), Z_GR // RET_W))
    tab = pl.BlockSpec((chunk, RET_QK_PAD), lambda b, c: (c, 0))
    st = pl.BlockSpec((nb, RET_H, RET_DK, RET_DV), lambda b, c: (b, 0, 0, 0))
    full = lambda shape: pl.BlockSpec(shape, lambda b, c: (0,) * len(shape))
    y, s_last = pl.pallas_call(
        functools.partial(_retention_kernel, nb),
        grid=(batch // nb, nc),
        in_specs=[zq, zk, zv, zg, tab, tab, st, full(intra.shape), full(qdec.shape), full(kdec.shape),
                  full(cdec.shape), full((1, RET_W))],
        out_specs=[pl.BlockSpec((rows, RET_W), lambda b, c: (row(b, c), 0)), st],
        out_shape=[jax.ShapeDtypeStruct((batch * seq, RET_W), F32),
                   jax.ShapeDtypeStruct((batch, RET_H, RET_DK, RET_DV), F32)],
        scratch_shapes=[pltpu.VMEM((nb, RET_H, RET_DK, RET_DV), F32)],
        compiler_params=_params(("arbitrary", "arbitrary")),
        name="retention",
    )(z, z, z, z, cos, sin, s0, intra, qdec, kdec, cdec, gn.reshape(1, RET_W))
    return y, s_last


def _retention_consts(seq):
    chunk = min(RET_CHUNK, seq)
    log_g = jnp.log1p(-jnp.exp2(-5.0 - jnp.arange(RET_H, dtype=F32)))
    idx = jnp.arange(chunk, dtype=F32)
    diff = idx[:, None] - idx[None, :]
    intra = jnp.where(diff >= 0, jnp.exp(log_g[:, None, None] * jnp.maximum(diff, 0.0)), 0.0)
    q_dec = jnp.exp(log_g[:, None] * (idx + 1.0))
    k_dec = jnp.exp(log_g[:, None] * (chunk - 1.0 - idx))
    chunk_dec = jnp.exp(log_g * chunk)
    expand = lambda t: jnp.pad(jnp.repeat(t.T, RET_DK, axis=1), ((0, 0), (0, RET_QK_PAD - RET_H * RET_DK)))
    cdec = jnp.broadcast_to(chunk_dec[:, None, None], (RET_H, RET_DK, RET_DV))
    return intra, expand(q_dec), expand(k_dec), cdec


def _mla_common(qa_ref, kva_ref, kra_ref, qg_ref, kvg_ref, wuq_ref, cq_ref, sq_ref, ckr_ref, skr_ref):
    cq = _rms(qa_ref[...], qg_ref[...]).astype(BF16)
    qh = _dot(cq, wuq_ref[...])
    cos = cq_ref[...]
    sin = sq_ref[...]
    scale = (MLA_NOPE + MLA_ROPE) ** -0.5
    q_heads = [_rope(qh[:, h * HEAD_PAD:(h + 1) * HEAD_PAD], cos, sin) * scale for h in range(MLA_H)]
    c_new = _rms(kva_ref[...], kvg_ref[...])
    kr_new = _rope(kra_ref[...], ckr_ref[...], skr_ref[...])
    return q_heads, c_new, kr_new


def _mla_prep_prompt_kernel(qa_ref, kva_ref, kra_ref, qg_ref, kvg_ref, wuq_ref, cq_ref, sq_ref, ckr_ref,
                            skr_ref, wk_ref, wvt_ref, c_ref, kr_ref, q_ref, k_ref, vt_ref):
    q_heads, c_new, kr_new = _mla_common(qa_ref, kva_ref, kra_ref, qg_ref, kvg_ref, wuq_ref, cq_ref, sq_ref,
                                         ckr_ref, skr_ref)
    c_ref[...] = c_new
    kr_ref[...] = kr_new[:, :MLA_ROPE]
    for h in range(MLA_H):
        q_ref[:, h * HEAD_PAD:(h + 1) * HEAD_PAD] = (q_heads[h] * LOG2_E).astype(BF16)
    ckr = jnp.concatenate([c_new, kr_new], axis=1).astype(BF16)
    k_ref[...] = _dot(ckr, wk_ref[...]).astype(BF16)
    vt = _dot_nt(wvt_ref[...], ckr[:, :MLA_KV_LORA])
    row = lax.broadcasted_iota(jnp.int32, vt.shape, 0) % VT_ROWS
    vt_ref[...] = jnp.where(row == MLA_DV, 1.0, vt).astype(BF16)


def _mla_prep_sample_kernel(qa_ref, kva_ref, kra_ref, qg_ref, kvg_ref, wuq_ref, cq_ref, sq_ref, ckr_ref,
                            skr_ref, wabs_ref, c_ref, kr_ref, q_ref):
    q_heads, c_new, kr_new = _mla_common(qa_ref, kva_ref, kra_ref, qg_ref, kvg_ref, wuq_ref, cq_ref, sq_ref,
                                         ckr_ref, skr_ref)
    c_ref[...] = c_new
    kr_ref[...] = kr_new[:, :MLA_ROPE]
    for h in range(MLA_H):
        q_ref[:, h * QABS_W:(h + 1) * QABS_W] = _dot(q_heads[h].astype(BF16), wabs_ref[h]).astype(BF16)


def _mla_prep(z, tabs, qg, kvg, wuq_p, extra, prompt):
    rows = z.shape[0]
    tm = min(512, rows)
    cq, sq, ckr, skr = tabs
    zc = lambda c, w: pl.BlockSpec((tm, w), lambda i, c=c, w=w: (i, c // w))
    tab = pl.BlockSpec((tm, LANES), lambda i: (i, 0))
    full = lambda shape: pl.BlockSpec(shape, lambda i: (0,) * len(shape))
    in_specs = [zc(Z_QA, MLA_Q_LORA), zc(Z_KVA, MLA_KV_LORA), zc(Z_KRA, LANES), full((1, MLA_Q_LORA)),
                full((1, MLA_KV_LORA)), full(wuq_p.shape), tab, tab, tab, tab] + [full(e.shape) for e in extra]
    rowblk = lambda w: pl.BlockSpec((tm, w), lambda i: (i, 0))
    out_specs = [rowblk(MLA_KV_LORA), rowblk(MLA_ROPE)]
    out_shape = [jax.ShapeDtypeStruct((rows, MLA_KV_LORA), F32), jax.ShapeDtypeStruct((rows, MLA_ROPE), F32)]
    if prompt:
        widths = [MLA_H * HEAD_PAD] * 2
        body = _mla_prep_prompt_kernel
    else:
        widths = [MLA_H * QABS_W]
        body = _mla_prep_sample_kernel
    out_specs += [rowblk(w) for w in widths]
    out_shape += [jax.ShapeDtypeStruct((rows, w), BF16) for w in widths]
    if prompt:
        out_specs.append(pl.BlockSpec((MLA_H * VT_ROWS, tm), lambda i: (0, i)))
        out_shape.append(jax.ShapeDtypeStruct((MLA_H * VT_ROWS, rows), BF16))
    return pl.pallas_call(
        body,
        grid=(rows // tm,),
        in_specs=in_specs,
        out_specs=out_specs,
        out_shape=out_shape,
        compiler_params=_params(("parallel",)),
        name="mla_prep_prompt" if prompt else "mla_prep_sample",
    )(z, z, z, qg.reshape(1, -1), kvg.reshape(1, -1), wuq_p, cq, sq, ckr, skr, *extra)


def _flash_kernel(qi_ref, ki_ref, q_ref, k_ref, vt_ref, o_ref, m_ref, acc_ref):
    qi = qi_ref[pl.program_id(0)]
    ki = ki_ref[pl.program_id(0)]
    tq = q_ref.shape[0]
    tk = k_ref.shape[0]
    first_diag = qi * (tq // tk)

    @pl.when(ki == 0)
    def _():
        m_ref[...] = jnp.full_like(m_ref, -jnp.inf)
        acc_ref[...] = jnp.zeros_like(acc_ref)

    def update(diag):
        first_strip = 0 if diag is None else diag
        chains = [(h, c) for h in range(MLA_H) for c in range(first_strip, tq // FLASH_STRIP)]

        def scores(h, c):
            hs = slice(h * HEAD_PAD, (h + 1) * HEAD_PAD)
            return _dot_nt(k_ref[:, hs], q_ref[c * FLASH_STRIP:(c + 1) * FLASH_STRIP, hs])

        pending = [scores(*ch) for ch in chains[:FLASH_AHEAD]]
        for n, (h, c) in enumerate(chains):
            cs = slice(c * FLASH_STRIP, (c + 1) * FLASH_STRIP)
            s = pending.pop(0)
            if n + FLASH_AHEAD < len(chains):
                pending.append(scores(*chains[n + FLASH_AHEAD]))
            if c == diag:
                key = lax.broadcasted_iota(jnp.int32, s.shape, 0)
                query = lax.broadcasted_iota(jnp.int32, s.shape, 1)
                s = jnp.where(key <= query, s, -jnp.inf)
            m_old = m_ref[h, :, cs]
            m_new = jnp.maximum(m_old, jnp.max(s, axis=0, keepdims=True))
            p = jnp.exp2(s - m_new)
            alpha = jnp.exp2(m_old - m_new)
            pv = _dot(vt_ref[h * VT_ROWS:(h + 1) * VT_ROWS, :], p.astype(BF16))
            acc_ref[h, :, cs] = alpha * acc_ref[h, :, cs] + pv
            m_ref[h, :, cs] = m_new

    @pl.when(ki < first_diag)
    def _():
        update(None)

    for d in range(tq // tk):
        @pl.when(ki == first_diag + d)
        def _(d=d):
            update(d)

    @pl.when(ki == first_diag + tq // tk - 1)
    def _():
        for h in range(MLA_H):
            acc = acc_ref[h]
            o_ref[:, h * MLA_DV:(h + 1) * MLA_DV] = (acc[:MLA_DV] / acc[MLA_DV:MLA_DV + 1]).T


def _flash(q, k, vt):
    rows = q.shape[0]
    tq = min(FLASH_TQ, rows)
    tk = FLASH_STRIP
    assert rows % tq == 0 and tq % tk == 0
    w = MLA_H * HEAD_PAD
    pairs = [(i, j) for i in range(rows // tq) for j in range((i + 1) * (tq // tk))]
    qi_tab = jnp.asarray([p[0] for p in pairs], jnp.int32)
    ki_tab = jnp.asarray([p[1] for p in pairs], jnp.int32)
    grid_spec = pltpu.PrefetchScalarGridSpec(
        num_scalar_prefetch=2,
        grid=(len(pairs),),
        in_specs=[pl.BlockSpec((tq, w), lambda s, qi, ki: (qi[s], 0)),
                  pl.BlockSpec((tk, w), lambda s, qi, ki: (ki[s], 0)),
                  pl.BlockSpec((MLA_H * VT_ROWS, tk), lambda s, qi, ki: (0, ki[s]))],
        out_specs=pl.BlockSpec((tq, MLA_W), lambda s, qi, ki: (qi[s], 0)),
        scratch_shapes=[pltpu.VMEM((MLA_H, 1, tq), F32), pltpu.VMEM((MLA_H, VT_ROWS, tq), F32)],
    )
    return pl.pallas_call(
        _flash_kernel,
        grid_spec=grid_spec,
        out_shape=jax.ShapeDtypeStruct((rows, MLA_W), F32),
        compiler_params=_params(("arbitrary",)),
        name="flash_prompt",
    )(qi_tab, ki_tab, q, k, vt)


def _paged_kernel(layer, pages, group, nch, pt_ref, q_ref, cn_ref, krn_ref, lat_hbm, krt_hbm, o_ref,
                  latbuf, krtbuf, lat_sem, krt_sem, m_ref, l_ref, acc_ref):
    s = pl.program_id(0)
    n_steps = pl.num_programs(0)
    ji = s % nch
    slot = s % PAGED_SLOTS

    def page_copies(step, dst_slot):
        out = []
        for i in range(pages):
            pid = pt_ref[step * pages + i]
            out.append(pltpu.make_async_copy(lat_hbm.at[layer, pid], latbuf.at[dst_slot, i], lat_sem.at[dst_slot]))
            out.append(pltpu.make_async_copy(krt_hbm.at[layer, pid], krtbuf.at[dst_slot, i], krt_sem.at[dst_slot]))
        return out

    def wait_slot(dst_slot):
        for i in range(pages):
            pltpu.make_async_copy(lat_hbm.at[layer, 0], latbuf.at[dst_slot, i], lat_sem.at[dst_slot]).wait()
            pltpu.make_async_copy(krt_hbm.at[layer, 0], krtbuf.at[dst_slot, i], krt_sem.at[dst_slot]).wait()

    @pl.when(s == 0)
    def _():
        for a in range(PAGED_AHEAD):
            for cp in page_copies(jnp.minimum(a, n_steps - 1), a):
                cp.start()

    wait_slot(slot)
    ahead_slot = (s + PAGED_AHEAD) % PAGED_SLOTS
    ahead_copies = page_copies(jnp.minimum(s + PAGED_AHEAD, n_steps - 1), ahead_slot)
    lat_refs = [latbuf.at[slot, i] for i in range(pages)]
    krt_refs = [krtbuf.at[slot, i] for i in range(pages)]

    @pl.when(ji == 0)
    def _():
        m_ref[...] = jnp.full_like(m_ref, -jnp.inf)
        l_ref[...] = jnp.zeros_like(l_ref)
        acc_ref[...] = jnp.zeros_like(acc_ref)

    q_lat = q_ref[:, :MLA_KV_LORA]
    q_pe = q_ref[:, MLA_KV_LORA:MLA_KV_LORA + MLA_ROPE]

    def local_softmax(s, values):
        m = jnp.max(s, axis=1, keepdims=True)
        p = jnp.exp(s - m)
        return m, jnp.sum(p, axis=1, keepdims=True), _dot(p.astype(BF16), values)

    def merge(parts):
        m_old = m_ref[...]
        m_new = m_old
        for m, _, _ in parts:
            m_new = jnp.maximum(m_new, m)
        alpha = jnp.exp(m_old - m_new)
        l_new = alpha * l_ref[...]
        acc = alpha * acc_ref[...]
        for m, l, o in parts:
            w = jnp.exp(m - m_new)
            l_new = l_new + w * l
            acc = acc + w * o
        m_ref[...] = m_new
        l_ref[...] = l_new
        acc_ref[...] = acc

    values, scores = [], []
    n_groups = pages // group
    per_group = len(ahead_copies) // n_groups
    for g in range(n_groups):
        ids = range(g * group, (g + 1) * group)
        c = jnp.concatenate([lat_refs[i][...].astype(BF16) for i in ids], axis=0)
        krt = jnp.concatenate([krt_refs[i][...].astype(BF16) for i in ids], axis=1)
        values.append(c)
        scores.append(_dot_nt(q_lat, c) + _dot(q_pe, krt))
        for cp in ahead_copies[g * per_group:(g + 1) * per_group]:
            cp.start()
    merge([local_softmax(s, c) for s, c in zip(scores, values)])

    @pl.when(ji == nch - 1)
    def _():
        c_new = cn_ref[...].astype(BF16)
        s_new = _dot_nt(q_lat, c_new) + _dot_nt(q_pe, krn_ref[...].astype(BF16))
        t_q = lax.broadcasted_iota(jnp.int32, s_new.shape, 0) % SUBLANES
        t_k = lax.broadcasted_iota(jnp.int32, s_new.shape, 1)
        merge([local_softmax(jnp.where(t_k <= t_q, s_new, -jnp.inf), c_new)])
        o_ref[...] = acc_ref[...] / l_ref[...]

    @pl.when(s == n_steps - 1)
    def _():
        for a in range(1, PAGED_AHEAD + 1):
            wait_slot((s + a) % PAGED_SLOTS)


def _paged_attention(layer, q, c_new, kr_new, cache_lat, cache_krt, page_table):
    batch, rows, _ = q.shape
    t_new = c_new.shape[1]
    n_pages = page_table.shape[1]
    page = cache_lat.shape[2]
    pages = min(64, n_pages)
    group = min(4, pages)
    nch = n_pages // pages

    per_b = lambda r, w: pl.BlockSpec((None, r, w), lambda s, pt: (s // nch, 0, 0))
    grid_spec = pltpu.PrefetchScalarGridSpec(
        num_scalar_prefetch=1,
        grid=(batch * nch,),
        in_specs=[per_b(rows, QABS_W), per_b(t_new, MLA_KV_LORA), per_b(t_new, MLA_ROPE),
                  pl.BlockSpec(memory_space=pl.ANY), pl.BlockSpec(memory_space=pl.ANY)],
        out_specs=per_b(rows, MLA_KV_LORA),
        scratch_shapes=[pltpu.VMEM((PAGED_SLOTS, pages, page, MLA_KV_LORA), F32),
                        pltpu.VMEM((PAGED_SLOTS, pages, MLA_ROPE, page), F32),
                        pltpu.SemaphoreType.DMA((PAGED_SLOTS,)), pltpu.SemaphoreType.DMA((PAGED_SLOTS,)),
                        pltpu.VMEM((rows, 1), F32), pltpu.VMEM((rows, 1), F32),
                        pltpu.VMEM((rows, MLA_KV_LORA), F32)],
    )
    return pl.pallas_call(
        functools.partial(_paged_kernel, layer, pages, group, nch),
        grid_spec=grid_spec,
        out_shape=jax.ShapeDtypeStruct((batch, rows, MLA_KV_LORA), F32),
        compiler_params=_params(("arbitrary",)),
        name="paged_sample",
    )(page_table.reshape(-1), q, c_new, kr_new, cache_lat, cache_krt)


def _uv_kernel(o_ref, w_ref, y_ref):
    nb = o_ref.shape[0]
    t = o_ref.shape[2]
    for h in range(MLA_H):
        o_h = o_ref[:, h].reshape(nb * t, MLA_KV_LORA).astype(BF16)
        y_ref[:, h * MLA_DV:(h + 1) * MLA_DV] = _dot(o_h, w_ref[h])


def _uv_proj(o_lat, w_uv_h):
    batch, _, t, _ = o_lat.shape
    return pl.pallas_call(
        _uv_kernel,
        grid=(1,),
        in_specs=[pl.BlockSpec(o_lat.shape, lambda i: (0, 0, 0, 0)),
                  pl.BlockSpec(w_uv_h.shape, lambda i: (0, 0, 0))],
        out_specs=pl.BlockSpec((batch * t, MLA_W), lambda i: (0, 0)),
        out_shape=jax.ShapeDtypeStruct((batch * t, MLA_W), F32),
        compiler_params=_params(("arbitrary",)),
        name="uv_proj",
    )(o_lat, w_uv_h)


def _out_proj_kernel(final, x_ref, ya_ref, yb_ref, oc_ref, gm_ref, w_ref, fg_ref, *out_refs):
    yc = oc_ref[...] * _silu(gm_ref[...])
    upd = _dot(ya_ref[...].astype(BF16), w_ref[0:LRU_W, :])
    upd = upd + _dot(yb_ref[...].astype(BF16), w_ref[LRU_W:LRU_W + RET_W, :])
    upd = upd + _dot(yc.astype(BF16), w_ref[LRU_W + RET_W:, :])
    x_new = x_ref[...] + upd
    out_refs[0][...] = x_new
    if final:
        out_refs[1][...] = _rms(x_new, fg_ref[...])


def _out_proj(x2d, y_a, y_b, o_c, z, w_out, final_g, final):
    rows = x2d.shape[0]
    tm = min(512, rows)
    rowblk = lambda w: pl.BlockSpec((tm, w), lambda i: (i, 0))
    full = lambda shape: pl.BlockSpec(shape, lambda i: (0,) * len(shape))
    n_out = 2 if final else 1
    outs = pl.pallas_call(
        functools.partial(_out_proj_kernel, final),
        grid=(rows // tm,),
        in_specs=[rowblk(D_MODEL), rowblk(LRU_W), rowblk(RET_W), rowblk(MLA_W),
                  pl.BlockSpec((tm, MLA_W), lambda i: (i, Z_GM // MLA_W)), full(w_out.shape), full((1, D_MODEL))],
        out_specs=[rowblk(D_MODEL)] * n_out,
        out_shape=[jax.ShapeDtypeStruct((rows, D_MODEL), F32)] * n_out,
        compiler_params=_params(("parallel",)),
        name="out_proj_final" if final else "out_proj",
    )(x2d, y_a, y_b, o_c, z, w_out, final_g.reshape(1, D_MODEL))
    return outs


def _pad_cols(w, width):
    return jnp.pad(w, ((0, 0), (0, width - w.shape[1])))


def _layout_w_in(w):
    splits = np.cumsum([0, LRU_W, LRU_W, RET_H * RET_DK, RET_H * RET_DK, RET_W, RET_W, MLA_Q_LORA, MLA_KV_LORA,
                        MLA_ROPE, MLA_W])
    seg = [w[:, splits[i]:splits[i + 1]] for i in range(10)]
    x_lru, g_lru, q_r, k_r, v_r, g_r, q_a, kv_a, kr_a, g_m = seg
    cols = [x_lru, g_lru, _pad_cols(q_r, RET_QK_PAD), _pad_cols(k_r, RET_QK_PAD), q_a, kv_a, v_r, g_r, g_m,
            _pad_cols(kr_a, LANES)]
    return jnp.concatenate(cols, axis=1).astype(BF16)


def _block_diag(w):
    n, c, d = w.shape
    eye = jnp.eye(n, dtype=w.dtype)
    return (eye[:, None, :, None] * w[:, :, None, :]).reshape(n * c, n * d)


def _layout_w_uq(w):
    per_head = w.reshape(MLA_Q_LORA, MLA_H, MLA_NOPE + MLA_ROPE)
    per_head = jnp.pad(per_head, ((0, 0), (0, 0), (0, HEAD_PAD - MLA_NOPE - MLA_ROPE)))
    return per_head.reshape(MLA_Q_LORA, MLA_H * HEAD_PAD).astype(BF16)


def _layout_w_key(w_uk):
    top = jnp.pad(w_uk, ((0, 0), (0, 0), (0, HEAD_PAD - MLA_NOPE)))
    rope_rows = jnp.zeros((LANES, MLA_H, HEAD_PAD), F32)
    eye = jnp.eye(MLA_ROPE, dtype=F32)
    rope_rows = rope_rows.at[:MLA_ROPE, :, MLA_NOPE:MLA_NOPE + MLA_ROPE].set(
        jnp.broadcast_to(eye[:, None, :], (MLA_ROPE, MLA_H, MLA_ROPE)))
    return jnp.concatenate([top, rope_rows], axis=0).reshape(MLA_KV_LORA + LANES, MLA_H * HEAD_PAD).astype(BF16)


def _layout_w_val_t(w_uv):
    w = jnp.pad(w_uv.transpose(1, 2, 0), ((0, 0), (0, VT_ROWS - MLA_DV), (0, 0)))
    return w.reshape(MLA_H * VT_ROWS, MLA_KV_LORA).astype(BF16)


def _layout_w_abs(w_uk):
    w = jnp.zeros((MLA_H, HEAD_PAD, QABS_W), F32)
    w = w.at[:, :MLA_NOPE, :MLA_KV_LORA].set(w_uk.transpose(1, 2, 0))
    w = w.at[:, MLA_NOPE:MLA_NOPE + MLA_ROPE, MLA_KV_LORA:MLA_KV_LORA + MLA_ROPE].set(
        jnp.broadcast_to(jnp.eye(MLA_ROPE, dtype=F32), (MLA_H, MLA_ROPE, MLA_ROPE)))
    return w.astype(BF16)


def _rope_tables(pos):
    half = MLA_ROPE // 2
    inv = ROPE_BASE ** (-jnp.arange(half, dtype=F32) / half)
    ang = pos[:, None] * inv[None, :]
    cos = jnp.cos(ang)
    sin = jnp.sin(ang)
    c32 = jnp.concatenate([cos, cos], axis=1)
    s32 = jnp.concatenate([-sin, sin], axis=1)
    t = pos.shape[0]
    ret_c = _pad_cols(jnp.tile(c32, (1, RET_H)), RET_QK_PAD)
    ret_s = _pad_cols(jnp.tile(s32, (1, RET_H)), RET_QK_PAD)
    q_c = jnp.concatenate([jnp.ones((t, MLA_NOPE), F32), c32, jnp.zeros((t, HEAD_PAD - MLA_NOPE - MLA_ROPE), F32)], axis=1)
    q_s = jnp.concatenate([jnp.zeros((t, MLA_NOPE), F32), s32, jnp.zeros((t, HEAD_PAD - MLA_NOPE - MLA_ROPE), F32)], axis=1)
    kr_c = _pad_cols(c32, LANES)
    kr_s = _pad_cols(s32, LANES)
    return (ret_c, ret_s), (q_c, q_s, kr_c, kr_s)


def _rope_placement():
    half = MLA_ROPE // 2

    def place(width, offsets, second_sign):
        c = np.zeros((half, width), np.float32)
        s = np.zeros((half, width), np.float32)
        for off in offsets:
            for j in range(half):
                c[j, off + j] = 1.0
                c[j, off + half + j] = 1.0
                s[j, off + j] = -1.0
                s[j, off + half + j] = second_sign
        return c, s

    ret = place(RET_QK_PAD, [h * RET_DK for h in range(RET_H)], 1.0)
    q = place(HEAD_PAD, [MLA_NOPE], 1.0)
    kr = place(LANES, [0], 1.0)
    q_base = np.zeros((1, HEAD_PAD), np.float32)
    q_base[0, :MLA_NOPE] = 1.0
    mats = [jnp.asarray(m, BF16) for pair in (ret, q, kr) for m in pair]
    return mats, jnp.asarray(q_base)


def _rope_table_kernel(cos_ref, sin_ref, e_rc, e_rs, e_qc, e_qs, e_kc, e_ks, qb_ref,
                       rc_ref, rs_ref, qc_ref, qs_ref, kc_ref, ks_ref):
    def split(x):
        hi = x.astype(BF16)
        rest = x - hi.astype(F32)
        mid = rest.astype(BF16)
        return hi, mid, (rest - mid.astype(F32)).astype(BF16)

    cos_parts = split(cos_ref[...])
    sin_parts = split(sin_ref[...])

    def expand(parts, e_ref):
        e = e_ref[...]
        return _dot_tn(parts[0], e) + _dot_tn(parts[1], e) + _dot_tn(parts[2], e)

    rc_ref[...] = expand(cos_parts, e_rc)
    rs_ref[...] = expand(sin_parts, e_rs)
    qc_ref[...] = expand(cos_parts, e_qc) + qb_ref[...]
    qs_ref[...] = expand(sin_parts, e_qs)
    kc_ref[...] = expand(cos_parts, e_kc)
    ks_ref[...] = expand(sin_parts, e_ks)


def _rope_tables_wide(pos):
    half = MLA_ROPE // 2
    t = pos.shape[0]
    tile = 2048
    assert t % tile == 0
    inv = ROPE_BASE ** (-jnp.arange(half, dtype=F32) / half)
    ang = inv[:, None] * pos[None, :]
    mats, q_base = _rope_placement()
    widths = [RET_QK_PAD, RET_QK_PAD, HEAD_PAD, HEAD_PAD, LANES, LANES]
    full = lambda a: pl.BlockSpec(a.shape, lambda i: (0,) * a.ndim)
    out = pl.pallas_call(
        _rope_table_kernel,
        grid=(t // tile,),
        in_specs=[pl.BlockSpec((half, tile), lambda i: (0, i))] * 2 + [full(m) for m in mats] + [full(q_base)],
        out_specs=[pl.BlockSpec((tile, w), lambda i: (i, 0)) for w in widths],
        out_shape=[jax.ShapeDtypeStruct((t, w), F32) for w in widths],
        compiler_params=_params(("parallel",)),
        name="rope_tables",
    )(jnp.cos(ang), jnp.sin(ang), *mats, q_base)
    return (out[0], out[1]), tuple(out[2:])


def kernel(x_prompt, x_sample, cache_mla_latent, cache_mla_krope, state_ret, state_lru_h, state_conv, page_table,
           norm_g, w_in, conv_w, conv_b, lru_wa, lru_ba, lru_wx, lru_bx, lru_lambda, ret_gn_g, q_norm_g, w_uq,
           kv_norm_g, w_uk, w_uv, w_out, final_norm_g):
    bp, tp, _ = x_prompt.shape
    bs, ts, _ = x_sample.shape
    depth = w_in.shape[0]
    past_len = page_table.shape[1] * cache_mla_latent.shape[2]
    assert bp == 1 and ts == SUBLANES

    pos_p = jnp.arange(tp, dtype=F32)
    pos_s = past_len + jnp.arange(ts, dtype=F32)
    ret_tab_p, mla_tab_p = _rope_tables_wide(pos_p)
    ret_tab_s, mla_tab_s = _rope_tables(pos_s)
    ret_const_p = _retention_consts(tp)
    ret_const_s = _retention_consts(ts)

    cache_krt = jnp.swapaxes(cache_mla_krope, 2, 3)

    xp = x_prompt.reshape(bp * tp, D_MODEL)
    xs = x_sample.reshape(bs * ts, D_MODEL)
    yp = ys = None
    outs = [[] for _ in range(10)]
    for l in range(depth):
        final = l == depth - 1
        w_in_p = _layout_w_in(w_in[l])
        lw = (conv_w[l], conv_b[l].reshape(1, -1), _block_diag(lru_wa[l]).astype(BF16), lru_ba[l].reshape(1, -1),
              _block_diag(lru_wx[l]).astype(BF16), lru_bx[l].reshape(1, -1), lru_lambda[l].reshape(1, -1))
        wuq_p = _layout_w_uq(w_uq[l])
        w_out_b = w_out[l].astype(BF16)

        z = _in_proj(xp, norm_g[l], w_in_p)
        y_a, h_last = _lru_prompt(z, lw)
        y_b, s_last = _retention_wide(z, *ret_tab_p, jnp.zeros((bp, RET_H, RET_DK, RET_DV), F32), ret_const_p,
                                      ret_gn_g[l], bp, tp)
        c_new, kr_new, q, k, vt = _mla_prep(z, mla_tab_p, q_norm_g[l], kv_norm_g[l], wuq_p,
                                            (_layout_w_key(w_uk[l]), _layout_w_val_t(w_uv[l])), True)
        o_c = _flash(q, k, vt)
        res = _out_proj(xp, y_a, y_b, o_c, z, w_out_b, final_norm_g, final)
        xp = res[0]
        if final:
            yp = res[1]
        conv_rows = z[:, Z_XLRU:Z_XLRU + LRU_W].reshape(bp, tp, LRU_W)[:, tp - (CONV_K - 1):, :]
        for lst, val in zip(outs[:5], (c_new.reshape(bp, tp, -1), kr_new.reshape(bp, tp, -1), s_last,
                                       h_last.reshape(bp, LRU_W), conv_rows)):
            lst.append(val)

        z = _in_proj(xs, norm_g[l], w_in_p)
        buf = jnp.pad(state_conv[l], ((0, 0), (SUBLANES - (CONV_K - 1), 0), (0, 0)))
        xprev = jnp.roll(buf, -1, axis=0).reshape(bs * ts, LRU_W)
        y_a, h_last = _lru_sample(z, xprev, state_lru_h[l], lw)
        y_b, s_last = _retention(z, *ret_tab_s, state_ret[l], ret_const_s, ret_gn_g[l], bs, ts)
        tabs_s = tuple(jnp.tile(t, (bs, 1)) for t in mla_tab_s)
        c_new, kr_new, q_abs = _mla_prep(z, tabs_s, q_norm_g[l], kv_norm_g[l], wuq_p, (_layout_w_abs(w_uk[l]),), False)
        q_abs = q_abs.reshape(bs, ts, MLA_H, QABS_W).transpose(0, 2, 1, 3).reshape(bs, MLA_H * ts, QABS_W)
        o_lat = _paged_attention(l, q_abs, c_new.reshape(bs, ts, -1), kr_new.reshape(bs, ts, -1),
                                 cache_mla_latent, cache_krt, page_table)
        o_c = _uv_proj(o_lat.reshape(bs, MLA_H, ts, MLA_KV_LORA), w_uv[l].transpose(1, 0, 2).astype(BF16))
        res = _out_proj(xs, y_a, y_b, o_c, z, w_out_b, final_norm_g, final)
        xs = res[0]
        if final:
            ys = res[1]
        conv_rows = z[:, Z_XLRU:Z_XLRU + LRU_W].reshape(bs, ts, LRU_W)[:, ts - (CONV_K - 1):, :]
        for lst, val in zip(outs[5:], (c_new.reshape(bs, ts, -1), kr_new.reshape(bs, ts, -1), s_last, h_last,
                                       conv_rows)):
            lst.append(val)

    return (yp.reshape(bp, tp, D_MODEL), ys.reshape(bs, ts, D_MODEL)) + tuple(jnp.stack(o) for o in outs)
```
